```python
import jax, jax.numpy as jnp
from jax import lax
import numpy as np

D_MODEL = 1024
BATCH = 8
SEQ = 4096
DEPTH = 1

ROPE_THETA = 500000.0
NORM_EPS = 1e-6
Q_BLOCK = 128

MLA_HEADS = 8
MLA_Q_RANK = 384
MLA_KV_RANK = 256
MLA_NOPE_DIM = 64
MLA_ROPE_DIM = 32
MLA_V_DIM = 64
MLA_WIDTH = MLA_HEADS * MLA_V_DIM

DIL_PATTERNS = ((128, 1), (512, 4), (2048, 16))
DIL_GROUPS = 3
DIL_HEADS_PER_GROUP = 8
DIL_HEADS = DIL_GROUPS * DIL_HEADS_PER_GROUP
DIL_HEAD_DIM = 64
DIL_ROPE_DIM = DIL_HEAD_DIM // 4
DIL_WIDTH = DIL_HEADS_PER_GROUP * DIL_HEAD_DIM

IN_SPLITS = (
    MLA_Q_RANK,
    MLA_KV_RANK,
    MLA_ROPE_DIM,
    3 * DIL_HEADS * DIL_HEAD_DIM,
    MLA_WIDTH,
    DIL_WIDTH,
    D_MODEL,
    D_MODEL,
)
IN_WIDTH = sum(IN_SPLITS)

kernel_name = "hybrid_mla_dilated_gated_parallel"


def rms_norm(x, g):
    xf = x.astype(jnp.float32)
    xf = xf * lax.rsqrt(jnp.mean(xf * xf, axis=-1, keepdims=True) + NORM_EPS)
    return (xf * g.astype(jnp.float32)).astype(x.dtype)


def rope_tables(positions, rot_dim):
    inv_freq = ROPE_THETA ** (-jnp.arange(0, rot_dim, 2, dtype=jnp.float32) / rot_dim)
    ang = positions.astype(jnp.float32)[..., None] * inv_freq
    return jnp.cos(ang), jnp.sin(ang)


def apply_rope(x, cos, sin):
    half = x.shape[-1] // 2
    xf = x.astype(jnp.float32)
    x1, x2 = xf[..., :half], xf[..., half:]
    return jnp.concatenate([x1 * cos - x2 * sin, x2 * cos + x1 * sin], axis=-1).astype(x.dtype)


def partial_rope(x, cos, sin, rot_dim):
    return jnp.concatenate([apply_rope(x[..., :rot_dim], cos, sin), x[..., rot_dim:]], axis=-1)


def mla_attention(c_q, c_kv, k_rope, q_norm_g, w_uq, kv_norm_g, w_ukv, cos, sin):
    B, S, _ = c_q.shape
    q = (rms_norm(c_q, q_norm_g) @ w_uq).reshape(B, S, MLA_HEADS, MLA_NOPE_DIM + MLA_ROPE_DIM)
    q_nope, q_rope = q[..., :MLA_NOPE_DIM], q[..., MLA_NOPE_DIM:]
    q_rope = apply_rope(q_rope, cos[:, :, None, :], sin[:, :, None, :])
    k_rope = apply_rope(k_rope, cos, sin)
    kv = (rms_norm(c_kv, kv_norm_g) @ w_ukv).reshape(B, S, MLA_HEADS, MLA_NOPE_DIM + MLA_V_DIM)
    k_nope, v = kv[..., :MLA_NOPE_DIM], kv[..., MLA_NOPE_DIM:]
    scale = (MLA_NOPE_DIM + MLA_ROPE_DIM) ** -0.5
    nb = S // Q_BLOCK
    qn_b = q_nope.reshape(B, nb, Q_BLOCK, MLA_HEADS, MLA_NOPE_DIM).transpose(1, 0, 2, 3, 4)
    qr_b = q_rope.reshape(B, nb, Q_BLOCK, MLA_HEADS, MLA_ROPE_DIM).transpose(1, 0, 2, 3, 4)
    key_idx = jnp.arange(S)

    def one_block(args):
        qn, qr, start = args
        s = (jnp.einsum('bqhd,bkhd->bhqk', qn, k_nope)
             + jnp.einsum('bqhr,bkr->bhqk', qr, k_rope)).astype(jnp.float32) * scale
        q_idx = start + jnp.arange(Q_BLOCK)
        s = jnp.where(key_idx[None, :] <= q_idx[:, None], s, -jnp.inf)
        p = jax.nn.softmax(s, axis=-1)
        return jnp.einsum('bhqk,bkhd->bqhd', p.astype(v.dtype), v)

    out = lax.map(one_block, (qn_b, qr_b, jnp.arange(nb) * Q_BLOCK))
    return out.transpose(1, 0, 2, 3, 4).reshape(B, S, MLA_WIDTH)


def dilated_group(q, k, v, window, dilation):
    B, S, H, Dh = q.shape
    L = S // dilation
    sub_window = window // dilation
    L_pad = -(-L // Q_BLOCK) * Q_BLOCK
    BD = B * dilation
    nb = L_pad // Q_BLOCK

    def to_sub(t):
        t = t.reshape(B, L, dilation, H, Dh).transpose(0, 2, 3, 1, 4).reshape(BD, H, L, Dh)
        return jnp.pad(t, ((0, 0), (0, 0), (0, L_pad - L), (0, 0)))

    qs, ks, vs = to_sub(q), to_sub(k), to_sub(v)
    qb = qs.reshape(BD, H, nb, Q_BLOCK, Dh)

    def band(t):
        tp = jnp.pad(t, ((0, 0), (0, 0), (Q_BLOCK, 0), (0, 0)))
        prev = tp[:, :, :L_pad].reshape(BD, H, nb, Q_BLOCK, Dh)
        cur = t.reshape(BD, H, nb, Q_BLOCK, Dh)
        return jnp.concatenate([prev, cur], axis=3)

    kb, vb = band(ks), band(vs)
    s = jnp.einsum('zhnqd,zhnkd->zhnqk', qb, kb).astype(jnp.float32) * (Dh ** -0.5)
    blk = jnp.arange(nb)[:, None, None] * Q_BLOCK
    qi = blk + jnp.arange(Q_BLOCK)[None, :, None]
    kj = blk - Q_BLOCK + jnp.arange(2 * Q_BLOCK)[None, None, :]
    dist = qi - kj
    mask = (dist >= 0) & (dist <= sub_window) & (kj >= 0)
    s = jnp.where(mask, s, -jnp.inf)
    m = jnp.max(s, axis=-1, keepdims=True)
    p = jnp.exp(s - m)
    denom = jnp.sum(p, axis=-1, keepdims=True)
    o = jnp.einsum('zhnqk,zhnkd->zhnqd', (p / denom).astype(v.dtype), vb)
    lse = (m + jnp.log(denom))[..., 0]
    o = o.reshape(BD, H, L_pad, Dh)[:, :, :L]
    o = o.reshape(B, dilation, H, L, Dh).transpose(0, 3, 1, 2, 4).reshape(B, S, H, Dh)
    lse = lse.reshape(BD, H, L_pad)[:, :, :L]
    lse = lse.reshape(B, dilation, H, L).transpose(0, 3, 1, 2).reshape(B, S, H)
    return o, lse


def dilated_attention(qkv, cos, sin):
    B, S, _ = qkv.shape
    qkv = qkv.reshape(B, S, 3, DIL_GROUPS, DIL_HEADS_PER_GROUP, DIL_HEAD_DIM)
    c, s_ = cos[:, :, None, None, :], sin[:, :, None, None, :]
    q = partial_rope(qkv[:, :, 0], c, s_, DIL_ROPE_DIM)
    k = partial_rope(qkv[:, :, 1], c, s_, DIL_ROPE_DIM)
    v = qkv[:, :, 2]
    outs, lses = [], []
    for g, (window, dilation) in enumerate(DIL_PATTERNS):
        o, lse = dilated_group(q[:, :, g], k[:, :, g], v[:, :, g], window, dilation)
        outs.append(o)
        lses.append(lse)
    alpha = jax.nn.softmax(jnp.stack(lses, axis=0), axis=0)
    out = jnp.sum(alpha[..., None].astype(v.dtype) * jnp.stack(outs, axis=0), axis=0)
    return out.reshape(B, S, DIL_WIDTH)


def setup_inputs(seed: int = 0) -> dict:
    key = jax.random.key(seed)
    ks = jax.random.split(key, 13)
    f32 = jnp.float32

    def w(k, shape, fan_in):
        return jax.random.normal(k, shape, f32) * (fan_in ** -0.5)

    def gain(k, n):
        return 1.0 + 0.02 * jax.random.normal(k, (DEPTH, n), f32)

    x = jax.random.normal(ks[0], (BATCH, SEQ, D_MODEL), f32)
    start = jax.random.randint(ks[1], (BATCH, 1), 0, 4096, dtype=jnp.int32)
    positions = start + jnp.arange(SEQ, dtype=jnp.int32)[None, :]
    return {
        "x": x,
        "positions": positions,
        "pre_norm_g": gain(ks[2], D_MODEL),
        "w_in": w(ks[3], (DEPTH, D_MODEL, IN_WIDTH), D_MODEL),
        "q_norm_g": gain(ks[4], MLA_Q_RANK),
        "w_uq": w(ks[5], (DEPTH, MLA_Q_RANK, MLA_HEADS * (MLA_NOPE_DIM + MLA_ROPE_DIM)), MLA_Q_RANK),
        "kv_norm_g": gain(ks[6], MLA_KV_RANK),
        "w_ukv": w(ks[7], (DEPTH, MLA_KV_RANK, MLA_HEADS * (MLA_NOPE_DIM + MLA_V_DIM)), MLA_KV_RANK),
        "w_proj_mla": w(ks[8], (DEPTH, MLA_WIDTH, D_MODEL), MLA_WIDTH),
        "w_proj_dil": w(ks[9], (DEPTH, DIL_WIDTH, D_MODEL), DIL_WIDTH),
        "w_out": w(ks[10], (DEPTH, D_MODEL, D_MODEL), D_MODEL),
        "post_norm_g": gain(ks[11], D_MODEL),
    }


def reference(x, positions, pre_norm_g, w_in, q_norm_g, w_uq, kv_norm_g, w_ukv,
              w_proj_mla, w_proj_dil, w_out, post_norm_g):
    cos_mla, sin_mla = rope_tables(positions, MLA_ROPE_DIM)
    cos_dil, sin_dil = rope_tables(positions, DIL_ROPE_DIM)
    split_at = [int(i) for i in np.cumsum(IN_SPLITS)[:-1]]
    for layer in range(DEPTH):
        h = rms_norm(x, pre_norm_g[layer])
        proj = h @ w_in[layer]
        c_q, c_kv, k_rope, qkv_dil, z_mla, z_dil, g_mla, g_dil = jnp.split(proj, split_at, axis=-1)
        y_mla = mla_attention(c_q, c_kv, k_rope, q_norm_g[layer], w_uq[layer],
                              kv_norm_g[layer], w_ukv[layer], cos_mla, sin_mla)
        y_dil = dilated_attention(qkv_dil, cos_dil, sin_dil)
        y_mla = (y_mla * jax.nn.silu(z_mla)) @ w_proj_mla[layer]
        y_dil = (y_dil * jax.nn.silu(z_dil)) @ w_proj_dil[layer]
        merged = jax.nn.sigmoid(g_mla) * y_mla + jax.nn.sigmoid(g_dil) * y_dil
        x = x + rms_norm(merged @ w_out[layer], post_norm_g[layer])
    return x
```

```python
import functools

import jax
import jax.numpy as jnp
import numpy as np
from jax import lax
from jax.experimental import pallas as pl
from jax.experimental.pallas import tpu as pltpu

F32 = jnp.float32
BF16 = jnp.bfloat16

D_MODEL = 1024
ROPE_THETA = 500000.0
NORM_EPS = 1e-6

MLA_HEADS = 8
MLA_Q_RANK = 384
MLA_KV_RANK = 256
MLA_NOPE = 64
MLA_ROPE = 32
MLA_V = 64
MLA_WIDTH = MLA_HEADS * MLA_V
MLA_HEAD_LANES = 128

DIL_DILATIONS = (1, 4, 16)
DIL_GROUPS = 3
DIL_HPG = 8
DIL_HEADS = DIL_GROUPS * DIL_HPG
DIL_DH = 64
DIL_ROPE = 16
DIL_WIDTH = DIL_HPG * DIL_DH
DIL_BLOCK = 128

LANES = 128
NEG_BIG = -1e30

VMEM_LIMIT = 56 * 1024 * 1024

LAT_WIDTH = MLA_Q_RANK + MLA_KV_RANK + LANES
QK_WIDTH = 2 * DIL_HEADS * DIL_DH
V_WIDTH = DIL_HEADS * DIL_DH
Z_WIDTH = MLA_WIDTH + DIL_WIDTH
G_WIDTH = 2 * D_MODEL
IN_TOTAL = LAT_WIDTH + QK_WIDTH + V_WIDTH + Z_WIDTH + G_WIDTH


def _rope_tables(pos_col, freq_row, lane_lo_a, lane_lo_b, width, period):
    ang = pos_col.astype(F32) * freq_row
    c = jnp.cos(ang)
    s = jnp.sin(ang)
    lane = lax.broadcasted_iota(jnp.int32, ang.shape, 1) % period
    in_a = (lane >= lane_lo_a) & (lane < lane_lo_a + width)
    sa = jnp.where(in_a, -s, 0.0)
    sb = jnp.where(in_a, 0.0, s)
    return c, sa, sb


def _apply_rope(a, c, sa, sb, shift):
    return a * c + pltpu.roll(a, LANES - shift, 1) * sa + pltpu.roll(a, shift, 1) * sb


def _in_proj_kernel(x_ref, pos_ref, g_ref, w_ref, fdil_ref,
                    lat_ref, qk_ref, v_ref, z_ref, gate_ref, *, chunk):
    x = x_ref[...]
    ms = jnp.mean(x * x, axis=-1, keepdims=True)
    h = (x * lax.rsqrt(ms + NORM_EPS) * g_ref[...]).astype(BF16)

    def proj(c0, width):
        return jnp.dot(h, w_ref[:, c0:c0 + width], preferred_element_type=F32)

    col = 0
    for c0 in range(0, LAT_WIDTH, chunk):
        w = min(chunk, LAT_WIDTH - c0)
        lat_ref[:, c0:c0 + w] = proj(col + c0, w)
    col += LAT_WIDTH

    c, sa, sb = _rope_tables(pos_ref[...], fdil_ref[...], 0, DIL_ROPE // 2, DIL_ROPE // 2, DIL_DH)
    q_scale = DIL_DH ** -0.5
    for c0 in range(0, QK_WIDTH, chunk):
        acc = proj(col + c0, chunk)
        for l0 in range(0, chunk, LANES):
            a = _apply_rope(acc[:, l0:l0 + LANES], c, sa, sb, DIL_ROPE // 2)
            if c0 + l0 < QK_WIDTH // 2:
                a = a * q_scale
            qk_ref[:, c0 + l0:c0 + l0 + LANES] = a.astype(BF16)
    col += QK_WIDTH

    for ref, width in ((v_ref, V_WIDTH), (z_ref, Z_WIDTH), (gate_ref, G_WIDTH)):
        for c0 in range(0, width, chunk):
            ref[:, c0:c0 + chunk] = proj(col + c0, chunk).astype(BF16)
        col += width


def _in_proj(x2, pos_col, g_pre, w_cat, f_dil, *, tm=512, chunk=512):
    t = x2.shape[0]
    row = lambda i: (i, 0)
    const = lambda i: (0, 0)
    return pl.pallas_call(
        functools.partial(_in_proj_kernel, chunk=chunk),
        grid=(t // tm,),
        in_specs=[
            pl.BlockSpec((tm, D_MODEL), row),
            pl.BlockSpec((tm, 1), row),
            pl.BlockSpec((1, D_MODEL), const),
            pl.BlockSpec((D_MODEL, IN_TOTAL), const, pipeline_mode=pl.Buffered(1)),
            pl.BlockSpec((1, LANES), const),
        ],
        out_specs=[
            pl.BlockSpec((tm, LAT_WIDTH), row),
            pl.BlockSpec((tm, QK_WIDTH), row),
            pl.BlockSpec((tm, V_WIDTH), row),
            pl.BlockSpec((tm, Z_WIDTH), row),
            pl.BlockSpec((tm, G_WIDTH), row),
        ],
        out_shape=[
            jax.ShapeDtypeStruct((t, LAT_WIDTH), F32),
            jax.ShapeDtypeStruct((t, QK_WIDTH), BF16),
            jax.ShapeDtypeStruct((t, V_WIDTH), BF16),
            jax.ShapeDtypeStruct((t, Z_WIDTH), BF16),
            jax.ShapeDtypeStruct((t, G_WIDTH), BF16),
        ],
        compiler_params=pltpu.CompilerParams(
            dimension_semantics=("arbitrary",), vmem_limit_bytes=VMEM_LIMIT),
    )(x2, pos_col, g_pre, w_cat, f_dil)


def _rms(x, g):
    ms = jnp.mean(x * x, axis=-1, keepdims=True)
    return x * lax.rsqrt(ms + NORM_EPS) * g


def _mla_up_kernel(lat_ref, pos_ref, qg_ref, kvg_ref, wq_ref, wk_ref, wv_ref, fmla_ref,
                   q_ref, k_ref, v_ref):
    cq = _rms(lat_ref[:, :MLA_Q_RANK], qg_ref[...]).astype(BF16)
    ckv = _rms(lat_ref[:, MLA_Q_RANK:MLA_Q_RANK + MLA_KV_RANK], kvg_ref[...]).astype(BF16)
    kr = lat_ref[:, MLA_Q_RANK + MLA_KV_RANK:]

    half = MLA_ROPE // 2
    c, sa, sb = _rope_tables(pos_ref[...], fmla_ref[...], MLA_NOPE, MLA_NOPE + half, half, LANES)
    kr = _apply_rope(kr, c, sa, sb, half)
    scale = (MLA_NOPE + MLA_ROPE) ** -0.5
    lane = lax.broadcasted_iota(jnp.int32, kr.shape, 1)
    ones_col = jnp.where(lane == MLA_V, 1.0, 0.0)

    for hd in range(MLA_HEADS):
        sl = slice(hd * MLA_HEAD_LANES, (hd + 1) * MLA_HEAD_LANES)
        q = jnp.dot(cq, wq_ref[:, sl], preferred_element_type=F32)
        q = _apply_rope(q, c, sa, sb, half) * scale
        q_ref[:, sl] = q.astype(BF16)
        k = jnp.dot(ckv, wk_ref[:, sl], preferred_element_type=F32) + kr
        k_ref[:, sl] = k.astype(BF16)
        v = jnp.dot(ckv, wv_ref[:, sl], preferred_element_type=F32) + ones_col
        v_ref[:, sl] = v.astype(BF16)


def _mla_up(lat, pos_col, qg, kvg, wq, wk, wv, f_mla, *, tm=512):
    t = lat.shape[0]
    row = lambda i: (i, 0)
    const = lambda i: (0, 0)
    width = MLA_HEADS * MLA_HEAD_LANES
    return pl.pallas_call(
        _mla_up_kernel,
        grid=(t // tm,),
        in_specs=[
            pl.BlockSpec((tm, LAT_WIDTH), row),
            pl.BlockSpec((tm, 1), row),
            pl.BlockSpec((1, MLA_Q_RANK), const),
            pl.BlockSpec((1, MLA_KV_RANK), const),
            pl.BlockSpec((MLA_Q_RANK, width), const),
            pl.BlockSpec((MLA_KV_RANK, width), const),
            pl.BlockSpec((MLA_KV_RANK, width), const),
            pl.BlockSpec((1, LANES), const),
        ],
        out_specs=[pl.BlockSpec((tm, width), row)] * 3,
        out_shape=[jax.ShapeDtypeStruct((t, width), BF16)] * 3,
        compiler_params=pltpu.CompilerParams(
            dimension_semantics=("arbitrary",), vmem_limit_bytes=VMEM_LIMIT),
    )(lat, pos_col, qg, kvg, wq, wk, wv, f_mla)


_NT = (((1,), (1,)), ((), ()))


def _mla_attn_kernel(q_ref, k_ref, v_ref, o_ref, *, tq, heads_per_step):
    qi = pl.program_id(2)
    for hd in range(heads_per_step):
        sl = slice(hd * MLA_HEAD_LANES, (hd + 1) * MLA_HEAD_LANES)
        q = q_ref[:, sl]

        def step(j, carry, masked):
            m, acc = carry
            start = pl.multiple_of(j * tq, tq)
            kb = k_ref[pl.ds(start, tq), sl]
            vb = v_ref[pl.ds(start, tq), sl]
            s = lax.dot_general(q, kb, _NT, preferred_element_type=F32)
            if masked:
                r = lax.broadcasted_iota(jnp.int32, s.shape, 0)
                cidx = lax.broadcasted_iota(jnp.int32, s.shape, 1)
                s = jnp.where(cidx <= r, s, NEG_BIG)
            m_new = jnp.maximum(m, jnp.max(s, axis=-1, keepdims=True))
            p = jnp.exp(s - m_new)
            acc = jnp.exp(m - m_new) * acc + jnp.dot(
                p.astype(BF16), vb, preferred_element_type=F32)
            return m_new, acc

        init = (jnp.full((tq, 1), NEG_BIG, F32), jnp.zeros((tq, MLA_HEAD_LANES), F32))
        carry = lax.fori_loop(0, qi, functools.partial(step, masked=False), init)
        _, acc = step(qi, carry, True)
        denom = acc[:, MLA_V:MLA_V + 1]
        o_ref[:, hd * MLA_V:(hd + 1) * MLA_V] = (acc[:, :MLA_V] / denom).astype(BF16)


def _mla_attn(q, k, v, batch, seq, *, tq=512, heads_per_step=2):
    width = MLA_HEADS * MLA_HEAD_LANES
    q3, k3, v3 = (a.reshape(batch, seq, width) for a in (q, k, v))
    hw = heads_per_step * MLA_HEAD_LANES
    out = pl.pallas_call(
        functools.partial(_mla_attn_kernel, tq=tq, heads_per_step=heads_per_step),
        grid=(batch, MLA_HEADS // heads_per_step, seq // tq),
        in_specs=[
            pl.BlockSpec((None, tq, hw), lambda b, hp, i: (b, i, hp)),
            pl.BlockSpec((None, seq, hw), lambda b, hp, i: (b, 0, hp)),
            pl.BlockSpec((None, seq, hw), lambda b, hp, i: (b, 0, hp)),
        ],
        out_specs=pl.BlockSpec((None, tq, heads_per_step * MLA_V), lambda b, hp, i: (b, i, hp)),
        out_shape=jax.ShapeDtypeStruct((batch, seq, MLA_WIDTH), BF16),
        compiler_params=pltpu.CompilerParams(
            dimension_semantics=("arbitrary", "arbitrary", "arbitrary"),
            vmem_limit_bytes=VMEM_LIMIT),
    )(q3, k3, v3)
    return out.reshape(batch * seq, MLA_WIDTH)


def _dil_attn_kernel(q_ref, kp_ref, kc_ref, vp_ref, vc_ref, o_ref, lse_ref):
    n = pl.program_id(2)
    blk = DIL_BLOCK
    r = lax.broadcasted_iota(jnp.int32, (blk, 2 * blk), 0)
    cidx = lax.broadcasted_iota(jnp.int32, (blk, 2 * blk), 1)
    dist = r + blk - cidx
    mask = (dist >= 0) & (dist <= blk) & ((cidx >= blk) | (n > 0))
    for hd in range(DIL_HPG):
        sl = slice(hd * DIL_DH, (hd + 1) * DIL_DH)
        q = q_ref[:, sl]
        k = jnp.concatenate([kp_ref[:, sl], kc_ref[:, sl]], axis=0)
        v = jnp.concatenate([vp_ref[:, sl], vc_ref[:, sl]], axis=0)
        s = lax.dot_general(q, k, _NT, preferred_element_type=F32)
        s = jnp.where(mask, s, NEG_BIG)
        m = jnp.max(s, axis=-1, keepdims=True)
        p = jnp.exp(s - m)
        denom = jnp.sum(p, axis=-1, keepdims=True)
        o = jnp.dot(p.astype(BF16), v, preferred_element_type=F32) / denom
        o_ref[:, sl] = o.astype(BF16)
        lse_ref[:, sl] = jnp.broadcast_to(m + jnp.log(denom), (blk, DIL_DH))


def _dil_attn(qk, v, group, batch, seq):
    d = DIL_DILATIONS[group]
    sub_len = seq // d
    blk = DIL_BLOCK
    qk3 = qk.reshape(batch, sub_len, d * QK_WIDTH)
    v3 = v.reshape(batch, sub_len, d * V_WIDTH)
    qk_blocks = QK_WIDTH // DIL_WIDTH
    v_blocks = V_WIDTH // DIL_WIDTH
    spec = lambda fn: pl.BlockSpec((None, blk, DIL_WIDTH), fn)
    prev = lambda n: jnp.maximum(n - 1, 0)
    o, lse = pl.pallas_call(
        _dil_attn_kernel,
        grid=(batch, d, sub_len // blk),
        in_specs=[
            spec(lambda b, r, n: (b, n, r * qk_blocks + group)),
            spec(lambda b, r, n: (b, prev(n), r * qk_blocks + DIL_GROUPS + group)),
            spec(lambda b, r, n: (b, n, r * qk_blocks + DIL_GROUPS + group)),
            spec(lambda b, r, n: (b, prev(n), r * v_blocks + group)),
            spec(lambda b, r, n: (b, n, r * v_blocks + group)),
        ],
        out_specs=[spec(lambda b, r, n: (b, n, r))] * 2,
        out_shape=[
            jax.ShapeDtypeStruct((batch, sub_len, d * DIL_WIDTH), BF16),
            jax.ShapeDtypeStruct((batch, sub_len, d * DIL_WIDTH), F32),
        ],
        compiler_params=pltpu.CompilerParams(
            dimension_semantics=("arbitrary", "arbitrary", "arbitrary"),
            vmem_limit_bytes=VMEM_LIMIT),
    )(qk3, qk3, qk3, v3, v3)
    return o.reshape(batch * seq, DIL_WIDTH), lse.reshape(batch * seq, DIL_WIDTH)


def _sigmoid(x):
    return 1.0 / (1.0 + jnp.exp(-x))


def _out_kernel(x_ref, ymla_ref, o0_ref, o1_ref, o2_ref, l0_ref, l1_ref, l2_ref,
                z_ref, gate_ref, wpm_ref, wpd_ref, wo_ref, pg_ref, out_ref):
    l0, l1, l2 = l0_ref[...], l1_ref[...], l2_ref[...]
    m = jnp.maximum(jnp.maximum(l0, l1), l2)
    e0, e1, e2 = jnp.exp(l0 - m), jnp.exp(l1 - m), jnp.exp(l2 - m)
    y_dil = (e0 * o0_ref[...].astype(F32) + e1 * o1_ref[...].astype(F32)
             + e2 * o2_ref[...].astype(F32)) / (e0 + e1 + e2)

    z_mla = z_ref[:, :MLA_WIDTH].astype(F32)
    z_dil = z_ref[:, MLA_WIDTH:].astype(F32)
    a = (ymla_ref[...].astype(F32) * (z_mla * _sigmoid(z_mla))).astype(BF16)
    b = (y_dil * (z_dil * _sigmoid(z_dil))).astype(BF16)
    pa = jnp.dot(a, wpm_ref[...], preferred_element_type=F32)
    pb = jnp.dot(b, wpd_ref[...], preferred_element_type=F32)
    merged = (_sigmoid(gate_ref[:, :D_MODEL].astype(F32)) * pa
              + _sigmoid(gate_ref[:, D_MODEL:].astype(F32)) * pb)
    u = jnp.dot(merged.astype(BF16), wo_ref[...], preferred_element_type=F32)
    out_ref[...] = x_ref[...] + _rms(u, pg_ref[...])


def _out_block(x2, y_mla, o_dil, lse_dil, z, gate, wpm, wpd, wo, pg, *, tm=512):
    t = x2.shape[0]
    row = lambda i: (i, 0)
    const = lambda i: (0, 0)
    rows = lambda w: pl.BlockSpec((tm, w), row)
    return pl.pallas_call(
        _out_kernel,
        grid=(t // tm,),
        in_specs=[rows(D_MODEL), rows(MLA_WIDTH)]
        + [rows(DIL_WIDTH)] * 6
        + [rows(Z_WIDTH), rows(G_WIDTH),
           pl.BlockSpec((MLA_WIDTH, D_MODEL), const),
           pl.BlockSpec((DIL_WIDTH, D_MODEL), const),
           pl.BlockSpec((D_MODEL, D_MODEL), const),
           pl.BlockSpec((1, D_MODEL), const)],
        out_specs=rows(D_MODEL),
        out_shape=jax.ShapeDtypeStruct((t, D_MODEL), F32),
        compiler_params=pltpu.CompilerParams(
            dimension_semantics=("arbitrary",), vmem_limit_bytes=VMEM_LIMIT),
    )(x2, y_mla, *o_dil, *lse_dil, z, gate, wpm, wpd, wo, pg)


def _inv_freq(rot_dim):
    return ROPE_THETA ** (-jnp.arange(0, rot_dim, 2, dtype=F32) / rot_dim)


def _lane_freqs():
    f_dil = jnp.zeros((DIL_DH,), F32).at[:DIL_ROPE].set(jnp.tile(_inv_freq(DIL_ROPE), 2))
    f_dil = jnp.tile(f_dil, LANES // DIL_DH)[None, :]
    f_mla = jnp.zeros((LANES,), F32).at[MLA_NOPE:MLA_NOPE + MLA_ROPE].set(
        jnp.tile(_inv_freq(MLA_ROPE), 2))[None, :]
    return f_dil, f_mla


def _layer_weights(w_in, w_uq, w_ukv):
    offs = np.cumsum([0, MLA_Q_RANK, MLA_KV_RANK, MLA_ROPE, 3 * V_WIDTH, MLA_WIDTH,
                      DIL_WIDTH, D_MODEL, D_MODEL])
    cq, ckv, kr, qkv, zm, zd, gm, gd = (w_in[:, offs[i]:offs[i + 1]] for i in range(8))
    zeros = lambda n: jnp.zeros((D_MODEL, n), w_in.dtype)
    kr_pad = jnp.concatenate([zeros(MLA_NOPE), kr, zeros(LANES - MLA_NOPE - MLA_ROPE)], axis=1)
    w_cat = jnp.concatenate([cq, ckv, kr_pad, qkv, zm, zd, gm, gd], axis=1).astype(BF16)

    pad_heads = lambda w, n: jnp.pad(w, ((0, 0), (0, 0), (0, MLA_HEAD_LANES - n)))
    wq = pad_heads(w_uq.reshape(MLA_Q_RANK, MLA_HEADS, MLA_NOPE + MLA_ROPE), MLA_NOPE + MLA_ROPE)
    wkv = w_ukv.reshape(MLA_KV_RANK, MLA_HEADS, MLA_NOPE + MLA_V)
    wk = pad_heads(wkv[:, :, :MLA_NOPE], MLA_NOPE)
    wv = pad_heads(wkv[:, :, MLA_NOPE:], MLA_V)
    flat = lambda w: w.reshape(w.shape[0], MLA_HEADS * MLA_HEAD_LANES).astype(BF16)
    return w_cat, flat(wq), flat(wk), flat(wv)


def kernel(x, positions, pre_norm_g, w_in, q_norm_g, w_uq, kv_norm_g, w_ukv,
           w_proj_mla, w_proj_dil, w_out, post_norm_g):
    batch, seq, _ = x.shape
    depth = w_in.shape[0]
    pos_col = positions.reshape(batch * seq, 1)
    f_dil, f_mla = _lane_freqs()
    x2 = x.reshape(batch * seq, D_MODEL)
    for layer in range(depth):
        w_cat, wq, wk, wv = _layer_weights(w_in[layer], w_uq[layer], w_ukv[layer])
        lat, qk, v_dil, z, gate = _in_proj(x2, pos_col, pre_norm_g[layer][None, :], w_cat, f_dil)
        q, k, v = _mla_up(lat, pos_col, q_norm_g[layer][None, :], kv_norm_g[layer][None, :],
                          wq, wk, wv, f_mla)
        y_mla = _mla_attn(q, k, v, batch, seq)
        dil = [_dil_attn(qk, v_dil, g, batch, seq) for g in range(DIL_GROUPS)]
        x2 = _out_block(x2, y_mla, [o for o, _ in dil], [l for _, l in dil], z, gate,
                        w_proj_mla[layer].astype(BF16), w_proj_dil[layer].astype(BF16),
                        w_out[layer].astype(BF16), post_norm_g[layer][None, :])
    return x2.reshape(batch, seq, D_MODEL)
```

```python
import functools
import math

import jax
import jax.numpy as jnp
import numpy as np
from jax import lax
from jax.experimental import pallas as pl
from jax.experimental.pallas import tpu as pltpu

F32 = jnp.float32
BF16 = jnp.bfloat16

D_MODEL = 1024
ROPE_THETA = 500000.0
NORM_EPS = 1e-6

MLA_HEADS = 8
MLA_Q_RANK = 384
MLA_KV_RANK = 256
MLA_NOPE = 64
MLA_ROPE = 32
MLA_V = 64
MLA_WIDTH = MLA_HEADS * MLA_V
MLA_HEAD_LANES = 128

DIL_DILATIONS = (1, 4, 16)
DIL_GROUPS = 3
DIL_HPG = 8
DIL_DH = 64
DIL_ROPE = 16
DIL_WIDTH = DIL_HPG * DIL_DH
DIL_BLOCK = 128
DIL_BLOCKS_PER_STEP = (4, 4, 2)

LANES = 128
NEG_BIG = -1e30
LOG2E = math.log2(math.e)

VMEM_LIMIT = 56 * 1024 * 1024

LAT_WIDTH = MLA_Q_RANK + MLA_KV_RANK + LANES
QKV_WIDTH = 3 * DIL_WIDTH
Z_WIDTH = MLA_WIDTH + DIL_WIDTH
G_WIDTH = 2 * D_MODEL
IN_TOTAL = LAT_WIDTH + DIL_GROUPS * QKV_WIDTH + Z_WIDTH + G_WIDTH
N_SLABS = D_MODEL // LANES


def _params(n_axes):
    return pltpu.CompilerParams(
        dimension_semantics=("arbitrary",) * n_axes, vmem_limit_bytes=VMEM_LIMIT)


def _rope_tables(pos_col, freq_row, lane_lo_a, width, period):
    ang = pos_col.astype(F32) * freq_row
    c = jnp.cos(ang)
    s = jnp.sin(ang)
    lane = lax.broadcasted_iota(jnp.int32, ang.shape, 1) % period
    in_a = (lane >= lane_lo_a) & (lane < lane_lo_a + width)
    return c, jnp.where(in_a, -s, 0.0), jnp.where(in_a, 0.0, s)


def _apply_rope(a, c, sa, sb, shift):
    return a * c + pltpu.roll(a, LANES - shift, 1) * sa + pltpu.roll(a, shift, 1) * sb


def _rms(x, g):
    ms = jnp.mean(x * x, axis=-1, keepdims=True)
    return x * lax.rsqrt(ms + NORM_EPS) * g


def _in_proj_kernel(x_ref, pos_ref, g_ref, w_ref, fdil_ref,
                    lat_ref, qkv0_ref, qkv1_ref, qkv2_ref, z_ref, gate_ref,
                    h_scr, tab_scr, hp_scr, *, tm):
    h = _rms(x_ref[...], g_ref[...])
    hb = h.astype(BF16)
    half = DIL_ROPE // 2
    tabs = _rope_tables(pos_ref[...], fdil_ref[...], 0, half, DIL_DH)
    q_scale = DIL_DH ** -0.5

    def proj(lhs, c0, width):
        return jnp.dot(lhs, w_ref[:, c0:c0 + width], preferred_element_type=F32)

    def qkv_pieces(lhs, c0, tables):
        c, sa, sb = tables
        for part in range(3):
            acc = proj(lhs, c0 + part * DIL_WIDTH, DIL_WIDTH)
            for l0 in range(0, DIL_WIDTH, LANES):
                a = acc[:, l0:l0 + LANES]
                if part < 2:
                    a = _apply_rope(a, c, sa, sb, half)
                if part == 0:
                    a = a * q_scale
                yield part * DIL_WIDTH + l0, a.astype(BF16)

    col = 0
    for c0 in range(0, LAT_WIDTH, 256):
        lat_ref[:, c0:c0 + 256] = proj(hb, col + c0, 256)
    col += LAT_WIDTH

    for off, piece in qkv_pieces(hb, col, tabs):
        qkv0_ref[0, :, off:off + LANES] = piece
    col += QKV_WIDTH

    for s in range(N_SLABS):
        h_scr[s] = h[:, s * LANES:(s + 1) * LANES]
    for i, t in enumerate(tabs):
        tab_scr[i] = t
    for d, out_ref in ((DIL_DILATIONS[1], qkv1_ref), (DIL_DILATIONS[2], qkv2_ref)):
        rows = tm // d
        for r in range(d):
            for s in range(N_SLABS):
                hp_scr[r * rows:(r + 1) * rows, s * LANES:(s + 1) * LANES] = (
                    h_scr[s, pl.ds(r, rows, stride=d), :].astype(BF16))
        ptabs = tuple(
            jnp.concatenate([tab_scr[i, pl.ds(r, rows, stride=d), :] for r in range(d)], axis=0)
            for i in range(3))
        for off, piece in qkv_pieces(hp_scr[...], col, ptabs):
            for r in range(d):
                out_ref[r, :, off:off + LANES] = piece[r * rows:(r + 1) * rows, :]
        col += QKV_WIDTH

    for ref, width in ((z_ref, Z_WIDTH), (gate_ref, G_WIDTH)):
        for c0 in range(0, width, 512):
            ref[:, c0:c0 + 512] = proj(hb, col + c0, 512).astype(BF16)
        col += width


def _in_proj(x2, pos_col, g_pre, w_cat, f_dil, batch, seq, *, tm=512):
    t = x2.shape[0]
    tiles_per_seq = seq // tm
    row = lambda i: (i, 0)
    const = lambda i: (0, 0)

    def qkv_spec(d):
        return pl.BlockSpec((None, d, tm // d, QKV_WIDTH),
                            lambda i: (i // tiles_per_seq, 0, i % tiles_per_seq, 0))

    return pl.pallas_call(
        functools.partial(_in_proj_kernel, tm=tm),
        grid=(t // tm,),
        in_specs=[
            pl.BlockSpec((tm, D_MODEL), row),
            pl.BlockSpec((tm, 1), row),
            pl.BlockSpec((1, D_MODEL), const),
            pl.BlockSpec((D_MODEL, IN_TOTAL), const, pipeline_mode=pl.Buffered(1)),
            pl.BlockSpec((1, LANES), const),
        ],
        out_specs=[pl.BlockSpec((tm, LAT_WIDTH), row)]
        + [qkv_spec(d) for d in DIL_DILATIONS]
        + [pl.BlockSpec((tm, Z_WIDTH), row), pl.BlockSpec((tm, G_WIDTH), row)],
        out_shape=[jax.ShapeDtypeStruct((t, LAT_WIDTH), F32)]
        + [jax.ShapeDtypeStruct((batch, d, seq // d, QKV_WIDTH), BF16) for d in DIL_DILATIONS]
        + [jax.ShapeDtypeStruct((t, Z_WIDTH), BF16), jax.ShapeDtypeStruct((t, G_WIDTH), BF16)],
        scratch_shapes=[
            pltpu.VMEM((N_SLABS, tm, LANES), F32),
            pltpu.VMEM((3, tm, LANES), F32),
            pltpu.VMEM((tm, D_MODEL), BF16),
        ],
        compiler_params=_params(1),
        name="in_proj",
    )(x2, pos_col, g_pre, w_cat, f_dil)


def _mla_up_kernel(lat_ref, pos_ref, qg_ref, kvg_ref, wq_ref, wk_ref, wv_ref, fmla_ref,
                   q_ref, k_ref, v_ref):
    cq = _rms(lat_ref[:, :MLA_Q_RANK], qg_ref[...]).astype(BF16)
    ckv = _rms(lat_ref[:, MLA_Q_RANK:MLA_Q_RANK + MLA_KV_RANK], kvg_ref[...]).astype(BF16)
    kr = lat_ref[:, MLA_Q_RANK + MLA_KV_RANK:]

    half = MLA_ROPE // 2
    c, sa, sb = _rope_tables(pos_ref[...], fmla_ref[...], MLA_NOPE, half, LANES)
    kr = _apply_rope(kr, c, sa, sb, half)
    scale = (MLA_NOPE + MLA_ROPE) ** -0.5 * LOG2E
    lane = lax.broadcasted_iota(jnp.int32, kr.shape, 1)
    ones_col = jnp.where(lane == MLA_V, 1.0, 0.0)

    for hd in range(MLA_HEADS):
        sl = slice(hd * MLA_HEAD_LANES, (hd + 1) * MLA_HEAD_LANES)
        q = jnp.dot(cq, wq_ref[:, sl], preferred_element_type=F32)
        q = _apply_rope(q, c, sa, sb, half) * scale
        q_ref[:, sl] = q.astype(BF16)
        k = jnp.dot(ckv, wk_ref[:, sl], preferred_element_type=F32) + kr
        k_ref[:, sl] = k.astype(BF16)
        v = jnp.dot(ckv, wv_ref[:, sl], preferred_element_type=F32) + ones_col
        v_ref[:, sl] = v.astype(BF16)


def _mla_up(lat, pos_col, qg, kvg, wq, wk, wv, f_mla, *, tm=512):
    t = lat.shape[0]
    row = lambda i: (i, 0)
    const = lambda i: (0, 0)
    width = MLA_HEADS * MLA_HEAD_LANES
    return pl.pallas_call(
        _mla_up_kernel,
        grid=(t // tm,),
        in_specs=[
            pl.BlockSpec((tm, LAT_WIDTH), row),
            pl.BlockSpec((tm, 1), row),
            pl.BlockSpec((1, MLA_Q_RANK), const),
            pl.BlockSpec((1, MLA_KV_RANK), const),
            pl.BlockSpec((MLA_Q_RANK, width), const),
            pl.BlockSpec((MLA_KV_RANK, width), const),
            pl.BlockSpec((MLA_KV_RANK, width), const),
            pl.BlockSpec((1, LANES), const),
        ],
        out_specs=[pl.BlockSpec((tm, width), row)] * 3,
        out_shape=[jax.ShapeDtypeStruct((t, width), BF16)] * 3,
        compiler_params=_params(1),
        name="mla_up",
    )(lat, pos_col, qg, kvg, wq, wk, wv, f_mla)


_NT = (((1,), (1,)), ((), ()))


def _mla_attn_kernel(q_ref, k_ref, v_ref, o_ref, *, tq, heads_per_step):
    qi = pl.program_id(2)
    slices = [slice(hd * MLA_HEAD_LANES, (hd + 1) * MLA_HEAD_LANES)
              for hd in range(heads_per_step)]
    qs = [q_ref[:, sl] for sl in slices]

    def step(j, carry, masked):
        start = pl.multiple_of(j * tq, tq)
        out = []
        for q, sl, (m, acc) in zip(qs, slices, carry):
            kb = k_ref[pl.ds(start, tq), sl]
            vb = v_ref[pl.ds(start, tq), sl]
            s = lax.dot_general(q, kb, _NT, preferred_element_type=F32)
            if masked:
                r = lax.broadcasted_iota(jnp.int32, s.shape, 0)
                cidx = lax.broadcasted_iota(jnp.int32, s.shape, 1)
                s = jnp.where(cidx <= r, s, NEG_BIG)
            m_new = jnp.maximum(m, jnp.max(s, axis=-1, keepdims=True))
            p = jnp.exp2(s - m_new)
            acc = jnp.exp2(m - m_new) * acc + jnp.dot(
                p.astype(BF16), vb, preferred_element_type=F32)
            out.append((m_new, acc))
        return tuple(out)

    init = tuple((jnp.full((tq, 1), NEG_BIG, F32), jnp.zeros((tq, MLA_HEAD_LANES), F32))
                 for _ in slices)
    carry = lax.fori_loop(0, qi, functools.partial(step, masked=False), init)
    carry = step(qi, carry, True)
    for hd, (_, acc) in enumerate(carry):
        denom = acc[:, MLA_V:MLA_V + 1]
        o_ref[:, hd * MLA_V:(hd + 1) * MLA_V] = (acc[:, :MLA_V] / denom).astype(BF16)


def _mla_attn(q, k, v, batch, seq, *, tq=512, heads_per_step=2):
    width = MLA_HEADS * MLA_HEAD_LANES
    q3, k3, v3 = (a.reshape(batch, seq, width) for a in (q, k, v))
    hw = heads_per_step * MLA_HEAD_LANES
    out = pl.pallas_call(
        functools.partial(_mla_attn_kernel, tq=tq, heads_per_step=heads_per_step),
        grid=(batch, MLA_HEADS // heads_per_step, seq // tq),
        in_specs=[
            pl.BlockSpec((None, tq, hw), lambda b, hp, i: (b, i, hp)),
            pl.BlockSpec((None, seq, hw), lambda b, hp, i: (b, 0, hp)),
            pl.BlockSpec((None, seq, hw), lambda b, hp, i: (b, 0, hp)),
        ],
        out_specs=pl.BlockSpec((None, tq, heads_per_step * MLA_V), lambda b, hp, i: (b, i, hp)),
        out_shape=jax.ShapeDtypeStruct((batch, seq, MLA_WIDTH), BF16),
        compiler_params=_params(3),
        name="mla_attn",
    )(q3, k3, v3)
    return out.reshape(batch * seq, MLA_WIDTH)


def _dil_attn_kernel(q_ref, kp_ref, kc_ref, vp_ref, vc_ref, o_ref, lse_ref,
                     kk_scr, vv_scr, *, nb):
    chunk = pl.program_id(2)
    blk = DIL_BLOCK
    kk_scr[:blk, :] = kp_ref[...]
    kk_scr[blk:, :] = kc_ref[...]
    vv_scr[:blk, :] = vp_ref[...]
    vv_scr[blk:, :] = vc_ref[...]

    r = lax.broadcasted_iota(jnp.int32, (blk, 2 * blk), 0)
    cidx = lax.broadcasted_iota(jnp.int32, (blk, 2 * blk), 1)
    dist = r + blk - cidx
    band = (dist >= 0) & (dist <= blk)
    lane = lax.broadcasted_iota(jnp.int32, (blk, LANES), 1)
    low = lane < DIL_DH

    for j in range(nb):
        rows = slice(j * blk, (j + 1) * blk)
        band_rows = slice(j * blk, (j + 2) * blk)
        if j == 0:
            mask = band & ((cidx >= blk) | (chunk > 0))
        else:
            mask = band
        for pair in range(DIL_HPG // 2):
            cols = slice(pair * LANES, (pair + 1) * LANES)
            q2 = q_ref[rows, cols]
            k2 = kk_scr[band_rows, cols]
            v2 = vv_scr[band_rows, cols]
            outs, lses = [], []
            for keep in (low, ~low):
                qh = jnp.where(keep, q2, jnp.zeros_like(q2))
                s = lax.dot_general(qh, k2, _NT, preferred_element_type=F32)
                s = jnp.where(mask, s, NEG_BIG)
                m = jnp.max(s, axis=-1, keepdims=True)
                p = jnp.exp(s - m)
                denom = jnp.sum(p, axis=-1, keepdims=True)
                outs.append(jnp.dot(p.astype(BF16), v2, preferred_element_type=F32) / denom)
                lses.append(m + jnp.log(denom))
            o_ref[rows, cols] = jnp.where(low, outs[0], outs[1]).astype(BF16)
            lse_ref[rows, cols] = jnp.where(low, lses[0], lses[1])


def _dil_attn(qkv, group):
    batch, d, sub_len, _ = qkv.shape
    nb = DIL_BLOCKS_PER_STEP[group]
    blk = DIL_BLOCK
    rows = nb * blk
    cur = lambda part: pl.BlockSpec((None, None, rows, DIL_WIDTH),
                                    lambda b, r, n: (b, r, n, part))
    prev = lambda part: pl.BlockSpec((None, None, blk, DIL_WIDTH),
                                     lambda b, r, n: (b, r, jnp.maximum(n * nb - 1, 0), part))
    out = pl.BlockSpec((None, None, rows, DIL_WIDTH), lambda b, r, n: (b, r, n, 0))
    return pl.pallas_call(
        functools.partial(_dil_attn_kernel, nb=nb),
        grid=(batch, d, sub_len // rows),
        in_specs=[cur(0), prev(1), cur(1), prev(2), cur(2)],
        out_specs=[out, out],
        out_shape=[
            jax.ShapeDtypeStruct((batch, d, sub_len, DIL_WIDTH), BF16),
            jax.ShapeDtypeStruct((batch, d, sub_len, DIL_WIDTH), F32),
        ],
        scratch_shapes=[pltpu.VMEM((rows + blk, DIL_WIDTH), BF16)] * 2,
        compiler_params=_params(3),
        name=f"dil_attn_{group}",
    )(qkv, qkv, qkv, qkv, qkv)


def _sigmoid(x):
    return 1.0 / (1.0 + jnp.exp(-x))


def _out_kernel(x_ref, ymla_ref, o0_ref, o1_ref, o2_ref, l0_ref, l1_ref, l2_ref,
                z_ref, gate_ref, wpm_ref, wpd_ref, wo_ref, pg_ref, out_ref, perm_scr, *, tm):
    n_sl = DIL_WIDTH // LANES

    def token_order(ref, d, slot):
        rows = tm // d
        for r in range(d):
            for s in range(n_sl):
                perm_scr[slot * n_sl + s, pl.ds(r, rows, stride=d), :] = (
                    ref[r, :, s * LANES:(s + 1) * LANES].astype(F32))
        return jnp.concatenate([perm_scr[slot * n_sl + s] for s in range(n_sl)], axis=1)

    d1, d2 = DIL_DILATIONS[1], DIL_DILATIONS[2]
    o0, l0 = o0_ref[0].astype(F32), l0_ref[0]
    o1, l1 = token_order(o1_ref, d1, 0), token_order(l1_ref, d1, 1)
    o2, l2 = token_order(o2_ref, d2, 2), token_order(l2_ref, d2, 3)
    m = jnp.maximum(jnp.maximum(l0, l1), l2)
    e0, e1, e2 = jnp.exp(l0 - m), jnp.exp(l1 - m), jnp.exp(l2 - m)
    y_dil = (e0 * o0 + e1 * o1 + e2 * o2) / (e0 + e1 + e2)

    z_mla = z_ref[:, :MLA_WIDTH].astype(F32)
    z_dil = z_ref[:, MLA_WIDTH:].astype(F32)
    a = (ymla_ref[...].astype(F32) * (z_mla * _sigmoid(z_mla))).astype(BF16)
    b = (y_dil * (z_dil * _sigmoid(z_dil))).astype(BF16)
    pa = jnp.dot(a, wpm_ref[...], preferred_element_type=F32)
    pb = jnp.dot(b, wpd_ref[...], preferred_element_type=F32)
    merged = (_sigmoid(gate_ref[:, :D_MODEL].astype(F32)) * pa
              + _sigmoid(gate_ref[:, D_MODEL:].astype(F32)) * pb)
    u = jnp.dot(merged.astype(BF16), wo_ref[...], preferred_element_type=F32)
    out_ref[...] = x_ref[...] + _rms(u, pg_ref[...])


def _out_block(x2, y_mla, o_dil, lse_dil, z, gate, wpm, wpd, wo, pg, seq, *, tm=512):
    t = x2.shape[0]
    tiles_per_seq = seq // tm
    row = lambda i: (i, 0)
    const = lambda i: (0, 0)
    rows = lambda w: pl.BlockSpec((tm, w), row)
    sub = lambda d: pl.BlockSpec((None, d, tm // d, DIL_WIDTH),
                                 lambda i: (i // tiles_per_seq, 0, i % tiles_per_seq, 0))
    subs = [sub(d) for d in DIL_DILATIONS]
    return pl.pallas_call(
        functools.partial(_out_kernel, tm=tm),
        grid=(t // tm,),
        in_specs=[rows(D_MODEL), rows(MLA_WIDTH)] + subs + subs
        + [rows(Z_WIDTH), rows(G_WIDTH),
           pl.BlockSpec((MLA_WIDTH, D_MODEL), const),
           pl.BlockSpec((DIL_WIDTH, D_MODEL), const),
           pl.BlockSpec((D_MODEL, D_MODEL), const),
           pl.BlockSpec((1, D_MODEL), const)],
        out_specs=rows(D_MODEL),
        out_shape=jax.ShapeDtypeStruct((t, D_MODEL), F32),
        scratch_shapes=[pltpu.VMEM((4 * DIL_WIDTH // LANES, tm, LANES), F32)],
        compiler_params=_params(1),
        name="out_block",
    )(x2, y_mla, *o_dil, *lse_dil, z, gate, wpm, wpd, wo, pg)


def _inv_freq(rot_dim):
    return ROPE_THETA ** (-jnp.arange(0, rot_dim, 2, dtype=F32) / rot_dim)


def _lane_freqs():
    f_dil = jnp.zeros((DIL_DH,), F32).at[:DIL_ROPE].set(jnp.tile(_inv_freq(DIL_ROPE), 2))
    f_dil = jnp.tile(f_dil, LANES // DIL_DH)[None, :]
    f_mla = jnp.zeros((LANES,), F32).at[MLA_NOPE:MLA_NOPE + MLA_ROPE].set(
        jnp.tile(_inv_freq(MLA_ROPE), 2))[None, :]
    return f_dil, f_mla


def _layer_weights(w_in, w_uq, w_ukv):
    offs = np.cumsum([0, MLA_Q_RANK, MLA_KV_RANK, MLA_ROPE, DIL_GROUPS * QKV_WIDTH, MLA_WIDTH,
                      DIL_WIDTH, D_MODEL, D_MODEL])
    cq, ckv, kr, qkv, zm, zd, gm, gd = (w_in[:, offs[i]:offs[i + 1]] for i in range(8))
    zeros = lambda n: jnp.zeros((D_MODEL, n), w_in.dtype)
    kr_pad = jnp.concatenate([zeros(MLA_NOPE), kr, zeros(LANES - MLA_NOPE - MLA_ROPE)], axis=1)
    qkv = qkv.reshape(D_MODEL, 3, DIL_GROUPS, DIL_WIDTH).transpose(0, 2, 1, 3)
    qkv = qkv.reshape(D_MODEL, DIL_GROUPS * QKV_WIDTH)
    w_cat = jnp.concatenate([cq, ckv, kr_pad, qkv, zm, zd, gm, gd], axis=1).astype(BF16)

    pad_heads = lambda w, n: jnp.pad(w, ((0, 0), (0, 0), (0, MLA_HEAD_LANES - n)))
    wq = pad_heads(w_uq.reshape(MLA_Q_RANK, MLA_HEADS, MLA_NOPE + MLA_ROPE), MLA_NOPE + MLA_ROPE)
    wkv = w_ukv.reshape(MLA_KV_RANK, MLA_HEADS, MLA_NOPE + MLA_V)
    wk = pad_heads(wkv[:, :, :MLA_NOPE], MLA_NOPE)
    wv = pad_heads(wkv[:, :, MLA_NOPE:], MLA_V)
    flat = lambda w: w.reshape(w.shape[0], MLA_HEADS * MLA_HEAD_LANES).astype(BF16)
    return w_cat, flat(wq), flat(wk), flat(wv)


def kernel(x, positions, pre_norm_g, w_in, q_norm_g, w_uq, kv_norm_g, w_ukv,
           w_proj_mla, w_proj_dil, w_out, post_norm_g):
    batch, seq, _ = x.shape
    depth = w_in.shape[0]
    pos_col = positions.reshape(batch * seq, 1)
    f_dil, f_mla = _lane_freqs()
    x2 = x.reshape(batch * seq, D_MODEL)
    for layer in range(depth):
        w_cat, wq, wk, wv = _layer_weights(w_in[layer], w_uq[layer], w_ukv[layer])
        lat, qkv0, qkv1, qkv2, z, gate = _in_proj(
            x2, pos_col, pre_norm_g[layer][None, :], w_cat, f_dil, batch, seq)
        q, k, v = _mla_up(lat, pos_col, q_norm_g[layer][None, :], kv_norm_g[layer][None, :],
                          wq, wk, wv, f_mla)
        y_mla = _mla_attn(q, k, v, batch, seq)
        dil = [_dil_attn(qkv, g) for g, qkv in enumerate((qkv0, qkv1, qkv2))]
        x2 = _out_block(x2, y_mla, [o for o, _ in dil], [l for _, l in dil], z, gate,
                        w_proj_mla[layer].astype(BF16), w_proj_dil[layer].astype(BF16),
                        w_out[layer].astype(BF16), post_norm_g[layer][None, :], seq)
    return x2.reshape(batch, seq, D_MODEL)
```

```python
import functools
import math

import jax
import jax.numpy as jnp
import numpy as np
from jax import lax
from jax.experimental import pallas as pl
from jax.experimental.pallas import tpu as pltpu

F32 = jnp.float32
BF16 = jnp.bfloat16

D_MODEL = 1024
ROPE_THETA = 500000.0
NORM_EPS = 1e-6

MLA_HEADS = 8
MLA_Q_RANK = 384
MLA_KV_RANK = 256
MLA_NOPE = 64
MLA_ROPE = 32
MLA_V = 64
MLA_WIDTH = MLA_HEADS * MLA_V
MLA_HEAD_LANES = 128

DIL_DILATIONS = (1, 4, 16)
DIL_GROUPS = 3
DIL_HPG = 8
DIL_DH = 64
DIL_ROPE = 16
DIL_WIDTH = DIL_HPG * DIL_DH
DIL_BLOCK = 128
DIL_BLOCKS_PER_STEP = (4, 4, 2)

LANES = 128
NEG_BIG = -1e30
LOG2E = math.log2(math.e)

VMEM_LIMIT = 56 * 1024 * 1024

LAT_WIDTH = MLA_Q_RANK + MLA_KV_RANK + LANES
QKV_WIDTH = 3 * DIL_WIDTH
Z_WIDTH = MLA_WIDTH + DIL_WIDTH
G_WIDTH = 2 * D_MODEL
IN_TOTAL = LAT_WIDTH + DIL_GROUPS * QKV_WIDTH + Z_WIDTH + G_WIDTH
N_SLABS = D_MODEL // LANES


def _params(n_axes):
    return pltpu.CompilerParams(
        dimension_semantics=("arbitrary",) * n_axes, vmem_limit_bytes=VMEM_LIMIT)


def _rope_tables(pos_col, freq_row, lane_lo_a, width, period):
    ang = pos_col.astype(F32) * freq_row
    c = jnp.cos(ang)
    s = jnp.sin(ang)
    lane = lax.broadcasted_iota(jnp.int32, ang.shape, 1) % period
    in_a = (lane >= lane_lo_a) & (lane < lane_lo_a + width)
    return c, jnp.where(in_a, -s, 0.0), jnp.where(in_a, 0.0, s)


def _apply_rope(a, c, sa, sb, shift):
    return a * c + pltpu.roll(a, LANES - shift, 1) * sa + pltpu.roll(a, shift, 1) * sb


def _rms(x, g):
    ms = jnp.mean(x * x, axis=-1, keepdims=True)
    return x * lax.rsqrt(ms + NORM_EPS) * g


def _in_proj_kernel(x_ref, pos_ref, g_ref, w_ref, fdil_ref,
                    lat_ref, qkv0_ref, qkv1_ref, qkv2_ref, z_ref, gate_ref,
                    h_scr, tab_scr, hp_scr, *, tm):
    h = _rms(x_ref[...], g_ref[...])
    hb = h.astype(BF16)
    half = DIL_ROPE // 2
    tabs = _rope_tables(pos_ref[...], fdil_ref[...], 0, half, DIL_DH)
    q_scale = DIL_DH ** -0.5

    def proj(lhs, c0, width):
        return jnp.dot(lhs, w_ref[:, c0:c0 + width], preferred_element_type=F32)

    def qkv_pieces(lhs, c0, tables):
        c, sa, sb = tables
        for part in range(3):
            acc = proj(lhs, c0 + part * DIL_WIDTH, DIL_WIDTH)
            for l0 in range(0, DIL_WIDTH, LANES):
                a = acc[:, l0:l0 + LANES]
                if part < 2:
                    a = _apply_rope(a, c, sa, sb, half)
                if part == 0:
                    a = a * q_scale
                yield part * DIL_WIDTH + l0, a.astype(BF16)

    col = 0
    for c0 in range(0, LAT_WIDTH, 256):
        lat_ref[:, c0:c0 + 256] = proj(hb, col + c0, 256)
    col += LAT_WIDTH

    for off, piece in qkv_pieces(hb, col, tabs):
        qkv0_ref[0, :, off:off + LANES] = piece
    col += QKV_WIDTH

    for s in range(N_SLABS):
        h_scr[s] = h[:, s * LANES:(s + 1) * LANES]
    for i, t in enumerate(tabs):
        tab_scr[i] = t
    for d, out_ref in ((DIL_DILATIONS[1], qkv1_ref), (DIL_DILATIONS[2], qkv2_ref)):
        rows = tm // d
        for r in range(d):
            for s in range(N_SLABS):
                hp_scr[r * rows:(r + 1) * rows, s * LANES:(s + 1) * LANES] = (
                    h_scr[s, pl.ds(r, rows, stride=d), :].astype(BF16))
        ptabs = tuple(
            jnp.concatenate([tab_scr[i, pl.ds(r, rows, stride=d), :] for r in range(d)], axis=0)
            for i in range(3))
        for off, piece in qkv_pieces(hp_scr[...], col, ptabs):
            for r in range(d):
                out_ref[r, :, off:off + LANES] = piece[r * rows:(r + 1) * rows, :]
        col += QKV_WIDTH

    for ref, width in ((z_ref, Z_WIDTH), (gate_ref, G_WIDTH)):
        for c0 in range(0, width, 512):
            ref[:, c0:c0 + 512] = proj(hb, col + c0, 512).astype(BF16)
        col += width


def _in_proj(x2, pos_col, g_pre, w_cat, f_dil, batch, seq, *, tm=512):
    t = x2.shape[0]
    tiles_per_seq = seq // tm
    row = lambda i: (i, 0)
    const = lambda i: (0, 0)

    def qkv_spec(d):
        return pl.BlockSpec((None, d, tm // d, QKV_WIDTH),
                            lambda i: (i // tiles_per_seq, 0, i % tiles_per_seq, 0))

    return pl.pallas_call(
        functools.partial(_in_proj_kernel, tm=tm),
        grid=(t // tm,),
        in_specs=[
            pl.BlockSpec((tm, D_MODEL), row),
            pl.BlockSpec((tm, 1), row),
            pl.BlockSpec((1, D_MODEL), const),
            pl.BlockSpec((D_MODEL, IN_TOTAL), const, pipeline_mode=pl.Buffered(1)),
            pl.BlockSpec((1, LANES), const),
        ],
        out_specs=[pl.BlockSpec((tm, LAT_WIDTH), row)]
        + [qkv_spec(d) for d in DIL_DILATIONS]
        + [pl.BlockSpec((tm, Z_WIDTH), row), pl.BlockSpec((tm, G_WIDTH), row)],
        out_shape=[jax.ShapeDtypeStruct((t, LAT_WIDTH), F32)]
        + [jax.ShapeDtypeStruct((batch, d, seq // d, QKV_WIDTH), BF16) for d in DIL_DILATIONS]
        + [jax.ShapeDtypeStruct((t, Z_WIDTH), BF16), jax.ShapeDtypeStruct((t, G_WIDTH), BF16)],
        scratch_shapes=[
            pltpu.VMEM((N_SLABS, tm, LANES), F32),
            pltpu.VMEM((3, tm, LANES), F32),
            pltpu.VMEM((tm, D_MODEL), BF16),
        ],
        compiler_params=_params(1),
        name="in_proj",
    )(x2, pos_col, g_pre, w_cat, f_dil)


def _mla_up_kernel(lat_ref, pos_ref, qg_ref, kvg_ref, wq_ref, wk_ref, wv_ref, fmla_ref,
                   q_ref, k_ref, v_ref):
    cq = _rms(lat_ref[:, :MLA_Q_RANK], qg_ref[...]).astype(BF16)
    ckv = _rms(lat_ref[:, MLA_Q_RANK:MLA_Q_RANK + MLA_KV_RANK], kvg_ref[...]).astype(BF16)
    kr = lat_ref[:, MLA_Q_RANK + MLA_KV_RANK:]

    half = MLA_ROPE // 2
    c, sa, sb = _rope_tables(pos_ref[...], fmla_ref[...], MLA_NOPE, half, LANES)
    kr = _apply_rope(kr, c, sa, sb, half)
    scale = (MLA_NOPE + MLA_ROPE) ** -0.5 * LOG2E
    lane = lax.broadcasted_iota(jnp.int32, kr.shape, 1)
    ones_col = jnp.where(lane == MLA_V, 1.0, 0.0)

    for hd in range(MLA_HEADS):
        sl = slice(hd * MLA_HEAD_LANES, (hd + 1) * MLA_HEAD_LANES)
        q = jnp.dot(cq, wq_ref[:, sl], preferred_element_type=F32)
        q = _apply_rope(q, c, sa, sb, half) * scale
        q_ref[:, sl] = q.astype(BF16)
        k = jnp.dot(ckv, wk_ref[:, sl], preferred_element_type=F32) + kr
        k_ref[:, sl] = k.astype(BF16)
        v = jnp.dot(ckv, wv_ref[:, sl], preferred_element_type=F32) + ones_col
        v_ref[:, sl] = v.astype(BF16)


def _mla_up(lat, pos_col, qg, kvg, wq, wk, wv, f_mla, *, tm=512):
    t = lat.shape[0]
    row = lambda i: (i, 0)
    const = lambda i: (0, 0)
    width = MLA_HEADS * MLA_HEAD_LANES
    return pl.pallas_call(
        _mla_up_kernel,
        grid=(t // tm,),
        in_specs=[
            pl.BlockSpec((tm, LAT_WIDTH), row),
            pl.BlockSpec((tm, 1), row),
            pl.BlockSpec((1, MLA_Q_RANK), const),
            pl.BlockSpec((1, MLA_KV_RANK), const),
            pl.BlockSpec((MLA_Q_RANK, width), const),
            pl.BlockSpec((MLA_KV_RANK, width), const),
            pl.BlockSpec((MLA_KV_RANK, width), const),
            pl.BlockSpec((1, LANES), const),
        ],
        out_specs=[pl.BlockSpec((tm, width), row)] * 3,
        out_shape=[jax.ShapeDtypeStruct((t, width), BF16)] * 3,
        compiler_params=_params(1),
        name="mla_up",
    )(lat, pos_col, qg, kvg, wq, wk, wv, f_mla)


_NT = (((1,), (1,)), ((), ()))


def _mla_attn_kernel(q_ref, k_ref, v_ref, o_ref, s_scr, m_scr, acc_scr, *, tq, heads_per_step):
    qi = pl.program_id(2)
    slices = [slice(hd * MLA_HEAD_LANES, (hd + 1) * MLA_HEAD_LANES)
              for hd in range(heads_per_step)]

    def scores(j, slot):
        start = pl.multiple_of(j * tq, tq)
        for hd, sl in enumerate(slices):
            s_scr[slot, hd] = lax.dot_general(
                q_ref[:, sl], k_ref[pl.ds(start, tq), sl], _NT, preferred_element_type=F32)

    def update(j, slot, masked):
        start = pl.multiple_of(j * tq, tq)
        for hd, sl in enumerate(slices):
            s = s_scr[slot, hd]
            if masked:
                r = lax.broadcasted_iota(jnp.int32, s.shape, 0)
                cidx = lax.broadcasted_iota(jnp.int32, s.shape, 1)
                s = jnp.where(cidx <= r, s, NEG_BIG)
            m = m_scr[hd]
            m_new = jnp.maximum(m, jnp.max(s, axis=-1, keepdims=True))
            p = jnp.exp2(s - m_new)
            acc_scr[hd] = jnp.exp2(m - m_new) * acc_scr[hd] + jnp.dot(
                p.astype(BF16), v_ref[pl.ds(start, tq), sl], preferred_element_type=F32)
            m_scr[hd] = m_new

    m_scr[...] = jnp.full(m_scr.shape, NEG_BIG, F32)
    acc_scr[...] = jnp.zeros(acc_scr.shape, F32)
    scores(0, 0)

    def pair(jj, carry):
        j0 = 2 * jj
        scores(j0 + 1, 1)
        update(j0, 0, False)
        scores(j0 + 2, 0)
        update(j0 + 1, 1, False)
        return carry

    lax.fori_loop(0, qi // 2, pair, 0)

    @pl.when(qi % 2 == 1)
    def _():
        scores(qi, 1)
        update(qi - 1, 0, False)

    update(qi, qi % 2, True)
    for hd in range(heads_per_step):
        acc = acc_scr[hd]
        denom = acc[:, MLA_V:MLA_V + 1]
        o_ref[:, hd * MLA_V:(hd + 1) * MLA_V] = (acc[:, :MLA_V] / denom).astype(BF16)


def _mla_attn(q, k, v, batch, seq, *, tq=512, heads_per_step=2):
    width = MLA_HEADS * MLA_HEAD_LANES
    q3, k3, v3 = (a.reshape(batch, seq, width) for a in (q, k, v))
    hw = heads_per_step * MLA_HEAD_LANES
    out = pl.pallas_call(
        functools.partial(_mla_attn_kernel, tq=tq, heads_per_step=heads_per_step),
        grid=(batch, MLA_HEADS // heads_per_step, seq // tq),
        in_specs=[
            pl.BlockSpec((None, tq, hw), lambda b, hp, i: (b, i, hp)),
            pl.BlockSpec((None, seq, hw), lambda b, hp, i: (b, 0, hp)),
            pl.BlockSpec((None, seq, hw), lambda b, hp, i: (b, 0, hp)),
        ],
        out_specs=pl.BlockSpec((None, tq, heads_per_step * MLA_V), lambda b, hp, i: (b, i, hp)),
        out_shape=jax.ShapeDtypeStruct((batch, seq, MLA_WIDTH), BF16),
        scratch_shapes=[
            pltpu.VMEM((2, heads_per_step, tq, tq), F32),
            pltpu.VMEM((heads_per_step, tq, 1), F32),
            pltpu.VMEM((heads_per_step, tq, MLA_HEAD_LANES), F32),
        ],
        compiler_params=_params(3),
        name="mla_attn",
    )(q3, k3, v3)
    return out.reshape(batch * seq, MLA_WIDTH)


def _dil_attn_kernel(q_ref, kp_ref, kc_ref, vp_ref, vc_ref, o_ref, lse_ref,
                     kk_scr, vv_scr, *, nb):
    chunk = pl.program_id(2)
    blk = DIL_BLOCK
    kk_scr[:blk, :] = kp_ref[...]
    kk_scr[blk:, :] = kc_ref[...]
    vv_scr[:blk, :] = vp_ref[...]
    vv_scr[blk:, :] = vc_ref[...]

    r = lax.broadcasted_iota(jnp.int32, (blk, 2 * blk), 0)
    cidx = lax.broadcasted_iota(jnp.int32, (blk, 2 * blk), 1)
    dist = r + blk - cidx
    band = (dist >= 0) & (dist <= blk)
    lane = lax.broadcasted_iota(jnp.int32, (blk, LANES), 1)
    low = lane < DIL_DH

    for j in range(nb):
        rows = slice(j * blk, (j + 1) * blk)
        band_rows = slice(j * blk, (j + 2) * blk)
        if j == 0:
            mask = band & ((cidx >= blk) | (chunk > 0))
        else:
            mask = band
        for pair in range(DIL_HPG // 2):
            cols = slice(pair * LANES, (pair + 1) * LANES)
            q2 = q_ref[rows, cols]
            k2 = kk_scr[band_rows, cols]
            v2 = vv_scr[band_rows, cols]
            outs, lses = [], []
            for keep in (low, ~low):
                qh = jnp.where(keep, q2, jnp.zeros_like(q2))
                s = lax.dot_general(qh, k2, _NT, preferred_element_type=F32)
                s = jnp.where(mask, s, NEG_BIG)
                m = jnp.max(s, axis=-1, keepdims=True)
                p = jnp.exp(s - m)
                denom = jnp.sum(p, axis=-1, keepdims=True)
                outs.append(jnp.dot(p.astype(BF16), v2, preferred_element_type=F32) / denom)
                lses.append(m + jnp.log(denom))
            o_ref[rows, cols] = jnp.where(low, outs[0], outs[1]).astype(BF16)
            lse_ref[rows, cols] = jnp.where(low, lses[0], lses[1])


def _dil_attn(qkv, group):
    batch, d, sub_len, _ = qkv.shape
    nb = DIL_BLOCKS_PER_STEP[group]
    blk = DIL_BLOCK
    rows = nb * blk
    cur = lambda part: pl.BlockSpec((None, None, rows, DIL_WIDTH),
                                    lambda b, r, n: (b, r, n, part))
    prev = lambda part: pl.BlockSpec((None, None, blk, DIL_WIDTH),
                                     lambda b, r, n: (b, r, jnp.maximum(n * nb - 1, 0), part))
    out = pl.BlockSpec((None, None, rows, DIL_WIDTH), lambda b, r, n: (b, r, n, 0))
    return pl.pallas_call(
        functools.partial(_dil_attn_kernel, nb=nb),
        grid=(batch, d, sub_len // rows),
        in_specs=[cur(0), prev(1), cur(1), prev(2), cur(2)],
        out_specs=[out, out],
        out_shape=[
            jax.ShapeDtypeStruct((batch, d, sub_len, DIL_WIDTH), BF16),
            jax.ShapeDtypeStruct((batch, d, sub_len, DIL_WIDTH), F32),
        ],
        scratch_shapes=[pltpu.VMEM((rows + blk, DIL_WIDTH), BF16)] * 2,
        compiler_params=_params(3),
        name=f"dil_attn_{group}",
    )(qkv, qkv, qkv, qkv, qkv)


def _sigmoid(x):
    return 1.0 / (1.0 + jnp.exp(-x))


def _out_kernel(x_ref, ymla_ref, o0_ref, o1_ref, o2_ref, l0_ref, l1_ref, l2_ref,
                z_ref, gate_ref, wpm_ref, wpd_ref, wo_ref, pg_ref, out_ref, perm_scr, *, tm):
    n_sl = DIL_WIDTH // LANES

    def token_order(ref, d, slot):
        rows = tm // d
        for r in range(d):
            for s in range(n_sl):
                perm_scr[slot * n_sl + s, pl.ds(r, rows, stride=d), :] = (
                    ref[r, :, s * LANES:(s + 1) * LANES].astype(F32))
        return jnp.concatenate([perm_scr[slot * n_sl + s] for s in range(n_sl)], axis=1)

    d1, d2 = DIL_DILATIONS[1], DIL_DILATIONS[2]
    o0, l0 = o0_ref[0].astype(F32), l0_ref[0]
    o1, l1 = token_order(o1_ref, d1, 0), token_order(l1_ref, d1, 1)
    o2, l2 = token_order(o2_ref, d2, 2), token_order(l2_ref, d2, 3)
    m = jnp.maximum(jnp.maximum(l0, l1), l2)
    e0, e1, e2 = jnp.exp(l0 - m), jnp.exp(l1 - m), jnp.exp(l2 - m)
    y_dil = (e0 * o0 + e1 * o1 + e2 * o2) / (e0 + e1 + e2)

    z_mla = z_ref[:, :MLA_WIDTH].astype(F32)
    z_dil = z_ref[:, MLA_WIDTH:].astype(F32)
    a = (ymla_ref[...].astype(F32) * (z_mla * _sigmoid(z_mla))).astype(BF16)
    b = (y_dil * (z_dil * _sigmoid(z_dil))).astype(BF16)
    pa = jnp.dot(a, wpm_ref[...], preferred_element_type=F32)
    pb = jnp.dot(b, wpd_ref[...], preferred_element_type=F32)
    merged = (_sigmoid(gate_ref[:, :D_MODEL].astype(F32)) * pa
              + _sigmoid(gate_ref[:, D_MODEL:].astype(F32)) * pb)
    u = jnp.dot(merged.astype(BF16), wo_ref[...], preferred_element_type=F32)
    out_ref[...] = x_ref[...] + _rms(u, pg_ref[...])


def _out_block(x2, y_mla, o_dil, lse_dil, z, gate, wpm, wpd, wo, pg, seq, *, tm=512):
    t = x2.shape[0]
    tiles_per_seq = seq // tm
    row = lambda i: (i, 0)
    const = lambda i: (0, 0)
    rows = lambda w: pl.BlockSpec((tm, w), row)
    sub = lambda d: pl.BlockSpec((None, d, tm // d, DIL_WIDTH),
                                 lambda i: (i // tiles_per_seq, 0, i % tiles_per_seq, 0))
    subs = [sub(d) for d in DIL_DILATIONS]
    return pl.pallas_call(
        functools.partial(_out_kernel, tm=tm),
        grid=(t // tm,),
        in_specs=[rows(D_MODEL), rows(MLA_WIDTH)] + subs + subs
        + [rows(Z_WIDTH), rows(G_WIDTH),
           pl.BlockSpec((MLA_WIDTH, D_MODEL), const),
           pl.BlockSpec((DIL_WIDTH, D_MODEL), const),
           pl.BlockSpec((D_MODEL, D_MODEL), const),
           pl.BlockSpec((1, D_MODEL), const)],
        out_specs=rows(D_MODEL),
        out_shape=jax.ShapeDtypeStruct((t, D_MODEL), F32),
        scratch_shapes=[pltpu.VMEM((4 * DIL_WIDTH // LANES, tm, LANES), F32)],
        compiler_params=_params(1),
        name="out_block",
    )(x2, y_mla, *o_dil, *lse_dil, z, gate, wpm, wpd, wo, pg)


def _inv_freq(rot_dim):
    return ROPE_THETA ** (-jnp.arange(0, rot_dim, 2, dtype=F32) / rot_dim)


def _lane_freqs():
    f_dil = jnp.zeros((DIL_DH,), F32).at[:DIL_ROPE].set(jnp.tile(_inv_freq(DIL_ROPE), 2))
    f_dil = jnp.tile(f_dil, LANES // DIL_DH)[None, :]
    f_mla = jnp.zeros((LANES,), F32).at[MLA_NOPE:MLA_NOPE + MLA_ROPE].set(
        jnp.tile(_inv_freq(MLA_ROPE), 2))[None, :]
    return f_dil, f_mla


def _layer_weights(w_in, w_uq, w_ukv):
    offs = np.cumsum([0, MLA_Q_RANK, MLA_KV_RANK, MLA_ROPE, DIL_GROUPS * QKV_WIDTH, MLA_WIDTH,
                      DIL_WIDTH, D_MODEL, D_MODEL])
    cq, ckv, kr, qkv, zm, zd, gm, gd = (w_in[:, offs[i]:offs[i + 1]] for i in range(8))
    zeros = lambda n: jnp.zeros((D_MODEL, n), w_in.dtype)
    kr_pad = jnp.concatenate([zeros(MLA_NOPE), kr, zeros(LANES - MLA_NOPE - MLA_ROPE)], axis=1)
    qkv = qkv.reshape(D_MODEL, 3, DIL_GROUPS, DIL_WIDTH).transpose(0, 2, 1, 3)
    qkv = qkv.reshape(D_MODEL, DIL_GROUPS * QKV_WIDTH)
    w_cat = jnp.concatenate([cq, ckv, kr_pad, qkv, zm, zd, gm, gd], axis=1).astype(BF16)

    pad_heads = lambda w, n: jnp.pad(w, ((0, 0), (0, 0), (0, MLA_HEAD_LANES - n)))
    wq = pad_heads(w_uq.reshape(MLA_Q_RANK, MLA_HEADS, MLA_NOPE + MLA_ROPE), MLA_NOPE + MLA_ROPE)
    wkv = w_ukv.reshape(MLA_KV_RANK, MLA_HEADS, MLA_NOPE + MLA_V)
    wk = pad_heads(wkv[:, :, :MLA_NOPE], MLA_NOPE)
    wv = pad_heads(wkv[:, :, MLA_NOPE:], MLA_V)
    flat = lambda w: w.reshape(w.shape[0], MLA_HEADS * MLA_HEAD_LANES).astype(BF16)
    return w_cat, flat(wq), flat(wk), flat(wv)


def kernel(x, positions, pre_norm_g, w_in, q_norm_g, w_uq, kv_norm_g, w_ukv,
           w_proj_mla, w_proj_dil, w_out, post_norm_g):
    batch, seq, _ = x.shape
    depth = w_in.shape[0]
    pos_col = positions.reshape(batch * seq, 1)
    f_dil, f_mla = _lane_freqs()
    x2 = x.reshape(batch * seq, D_MODEL)
    for layer in range(depth):
        w_cat, wq, wk, wv = _layer_weights(w_in[layer], w_uq[layer], w_ukv[layer])
        lat, qkv0, qkv1, qkv2, z, gate = _in_proj(
            x2, pos_col, pre_norm_g[layer][None, :], w_cat, f_dil, batch, seq)
        q, k, v = _mla_up(lat, pos_col, q_norm_g[layer][None, :], kv_norm_g[layer][None, :],
                          wq, wk, wv, f_mla)
        y_mla = _mla_attn(q, k, v, batch, seq)
        dil = [_dil_attn(qkv, g) for g, qkv in enumerate((qkv0, qkv1, qkv2))]
        x2 = _out_block(x2, y_mla, [o for o, _ in dil], [l for _, l in dil], z, gate,
                        w_proj_mla[layer].astype(BF16), w_proj_dil[layer].astype(BF16),
                        w_out[layer].astype(BF16), post_norm_g[layer][None, :], seq)
    return x2.reshape(batch, seq, D_MODEL)
```

```python
import functools
import math

import jax
import jax.numpy as jnp
import numpy as np
from jax import lax
from jax.experimental import pallas as pl
from jax.experimental.pallas import tpu as pltpu

F32 = jnp.float32
BF16 = jnp.bfloat16

D_MODEL = 1024
ROPE_THETA = 500000.0
NORM_EPS = 1e-6

MLA_HEADS = 8
MLA_Q_RANK = 384
MLA_KV_RANK = 256
MLA_NOPE = 64
MLA_ROPE = 32
MLA_V = 64
MLA_WIDTH = MLA_HEADS * MLA_V
MLA_HEAD_LANES = 128

DIL_DILATIONS = (1, 4, 16)
DIL_GROUPS = 3
DIL_HPG = 8
DIL_DH = 64
DIL_ROPE = 16
DIL_WIDTH = DIL_HPG * DIL_DH
DIL_BLOCK = 128
DIL_BLOCKS_PER_STEP = (4, 4, 2)

LANES = 128
NEG_BIG = -1e30
LOG2E = math.log2(math.e)

VMEM_LIMIT = 56 * 1024 * 1024

LAT_WIDTH = MLA_Q_RANK + MLA_KV_RANK + LANES
QKV_WIDTH = 3 * DIL_WIDTH
Z_WIDTH = MLA_WIDTH + DIL_WIDTH
G_WIDTH = 2 * D_MODEL
IN_TOTAL = LAT_WIDTH + DIL_GROUPS * QKV_WIDTH + Z_WIDTH + G_WIDTH
N_SLABS = D_MODEL // LANES


def _params(n_axes):
    return pltpu.CompilerParams(
        dimension_semantics=("arbitrary",) * n_axes, vmem_limit_bytes=VMEM_LIMIT)


def _trig_tables(pos_col, freq_row):
    ang = pos_col.astype(F32) * freq_row
    return jnp.cos(ang), jnp.sin(ang)


def _rope_tables(cos, sin, lane_lo, width, period):
    lane = lax.broadcasted_iota(jnp.int32, cos.shape, 1) % period
    in_a = (lane >= lane_lo) & (lane < lane_lo + width)
    in_b = (lane >= lane_lo + width) & (lane < lane_lo + 2 * width)
    c = jnp.where(in_a | in_b, cos, 1.0)
    return c, jnp.where(in_a, -sin, 0.0), jnp.where(in_b, sin, 0.0)


def _apply_rope(a, c, sa, sb, shift):
    return a * c + pltpu.roll(a, LANES - shift, 1) * sa + pltpu.roll(a, shift, 1) * sb


def _rms(x, g):
    ms = jnp.mean(x * x, axis=-1, keepdims=True)
    return x * lax.rsqrt(ms + NORM_EPS) * g


def _in_proj_kernel(x_ref, pos_ref, g_ref, w_ref, freq_ref, qg_ref, kvg_ref,
                    wq_ref, wk_ref, wv_ref,
                    q_ref, k_ref, v_ref, qkv0_ref, qkv1_ref, qkv2_ref, z_ref, gate_ref,
                    h_scr, tab_scr, hp_scr, *, tm):
    h = _rms(x_ref[...], g_ref[...])
    hb = h.astype(BF16)
    cos, sin = _trig_tables(pos_ref[...], freq_ref[...])

    def proj(lhs, c0, width):
        return jnp.dot(lhs, w_ref[:, c0:c0 + width], preferred_element_type=F32)

    lat = proj(hb, 0, LAT_WIDTH)
    cq = _rms(lat[:, :MLA_Q_RANK], qg_ref[...]).astype(BF16)
    ckv = _rms(lat[:, MLA_Q_RANK:MLA_Q_RANK + MLA_KV_RANK], kvg_ref[...]).astype(BF16)
    mhalf = MLA_ROPE // 2
    mtabs = _rope_tables(cos, sin, MLA_NOPE, mhalf, LANES)
    kr = _apply_rope(lat[:, MLA_Q_RANK + MLA_KV_RANK:], *mtabs, mhalf)
    q_scale = (MLA_NOPE + MLA_ROPE) ** -0.5 * LOG2E
    lane = lax.broadcasted_iota(jnp.int32, kr.shape, 1)
    ones_col = jnp.where(lane == MLA_V, 1.0, 0.0)
    q_all = jnp.dot(cq, wq_ref[...], preferred_element_type=F32)
    k_all = jnp.dot(ckv, wk_ref[...], preferred_element_type=F32)
    v_all = jnp.dot(ckv, wv_ref[...], preferred_element_type=F32)
    for hd in range(MLA_HEADS):
        sl = slice(hd * MLA_HEAD_LANES, (hd + 1) * MLA_HEAD_LANES)
        q_ref[:, sl] = (_apply_rope(q_all[:, sl], *mtabs, mhalf) * q_scale).astype(BF16)
        k_ref[:, sl] = (k_all[:, sl] + kr).astype(BF16)
        v_ref[:, sl] = (v_all[:, sl] + ones_col).astype(BF16)

    dhalf = DIL_ROPE // 2
    first = lax.broadcasted_iota(jnp.int32, cos.shape, 1) < DIL_DH
    dtabs = _rope_tables(jnp.where(first, cos, pltpu.roll(cos, DIL_DH, 1)),
                         jnp.where(first, sin, pltpu.roll(sin, DIL_DH, 1)), 0, dhalf, DIL_DH)
    d_scale = DIL_DH ** -0.5

    def qkv_pieces(lhs, c0, tables):
        for part in range(3):
            acc = proj(lhs, c0 + part * DIL_WIDTH, DIL_WIDTH)
            for l0 in range(0, DIL_WIDTH, LANES):
                a = acc[:, l0:l0 + LANES]
                if part < 2:
                    a = _apply_rope(a, *tables, dhalf)
                if part == 0:
                    a = a * d_scale
                yield part * DIL_WIDTH + l0, a.astype(BF16)

    col = LAT_WIDTH
    for off, piece in qkv_pieces(hb, col, dtabs):
        qkv0_ref[0, :, off:off + LANES] = piece
    col += QKV_WIDTH

    for s in range(N_SLABS):
        h_scr[s] = h[:, s * LANES:(s + 1) * LANES]
    for i, t in enumerate(dtabs):
        tab_scr[i] = t
    for d, out_ref in ((DIL_DILATIONS[1], qkv1_ref), (DIL_DILATIONS[2], qkv2_ref)):
        rows = tm // d
        for r in range(d):
            for s in range(N_SLABS):
                hp_scr[r * rows:(r + 1) * rows, s * LANES:(s + 1) * LANES] = (
                    h_scr[s, pl.ds(r, rows, stride=d), :].astype(BF16))
        ptabs = tuple(
            jnp.concatenate([tab_scr[i, pl.ds(r, rows, stride=d), :] for r in range(d)], axis=0)
            for i in range(3))
        for off, piece in qkv_pieces(hp_scr[...], col, ptabs):
            for r in range(d):
                out_ref[r, :, off:off + LANES] = piece[r * rows:(r + 1) * rows, :]
        col += QKV_WIDTH

    for ref, width in ((z_ref, Z_WIDTH), (gate_ref, G_WIDTH)):
        for c0 in range(0, width, 512):
            ref[:, c0:c0 + 512] = proj(hb, col + c0, 512).astype(BF16)
        col += width


def _in_proj(x2, pos_col, g_pre, w_cat, freqs, qg, kvg, wq, wk, wv, batch, seq, *, tm=512):
    t = x2.shape[0]
    tiles_per_seq = seq // tm
    mla_width = MLA_HEADS * MLA_HEAD_LANES
    row = lambda i: (i, 0)
    const = lambda i: (0, 0)
    resident = lambda shape: pl.BlockSpec(shape, const, pipeline_mode=pl.Buffered(1))

    def qkv_spec(d):
        return pl.BlockSpec((None, d, tm // d, QKV_WIDTH),
                            lambda i: (i // tiles_per_seq, 0, i % tiles_per_seq, 0))

    return pl.pallas_call(
        functools.partial(_in_proj_kernel, tm=tm),
        grid=(t // tm,),
        in_specs=[
            pl.BlockSpec((tm, D_MODEL), row),
            pl.BlockSpec((tm, 1), row),
            resident((1, D_MODEL)),
            resident((D_MODEL, IN_TOTAL)),
            resident((1, LANES)),
            resident((1, MLA_Q_RANK)),
            resident((1, MLA_KV_RANK)),
            resident((MLA_Q_RANK, mla_width)),
            resident((MLA_KV_RANK, mla_width)),
            resident((MLA_KV_RANK, mla_width)),
        ],
        out_specs=[pl.BlockSpec((tm, mla_width), row)] * 3
        + [qkv_spec(d) for d in DIL_DILATIONS]
        + [pl.BlockSpec((tm, Z_WIDTH), row), pl.BlockSpec((tm, G_WIDTH), row)],
        out_shape=[jax.ShapeDtypeStruct((t, mla_width), BF16)] * 3
        + [jax.ShapeDtypeStruct((batch, d, seq // d, QKV_WIDTH), BF16) for d in DIL_DILATIONS]
        + [jax.ShapeDtypeStruct((t, Z_WIDTH), BF16), jax.ShapeDtypeStruct((t, G_WIDTH), BF16)],
        scratch_shapes=[
            pltpu.VMEM((N_SLABS, tm, LANES), F32),
            pltpu.VMEM((3, tm, LANES), F32),
            pltpu.VMEM((tm, D_MODEL), BF16),
        ],
        compiler_params=_params(1),
        name="in_proj",
    )(x2, pos_col, g_pre, w_cat, freqs, qg, kvg, wq, wk, wv)


_NT = (((1,), (1,)), ((), ()))


def _mla_attn_kernel(q_ref, k_ref, v_ref, o_ref, s_scr, m_scr, acc_scr, *, tq, heads_per_step):
    qi = pl.program_id(2)
    slices = [slice(hd * MLA_HEAD_LANES, (hd + 1) * MLA_HEAD_LANES)
              for hd in range(heads_per_step)]

    def scores(j, slot):
        start = pl.multiple_of(j * tq, tq)
        for hd, sl in enumerate(slices):
            s_scr[slot, hd] = lax.dot_general(
                q_ref[:, sl], k_ref[pl.ds(start, tq), sl], _NT, preferred_element_type=F32)

    def update(j, slot, masked):
        start = pl.multiple_of(j * tq, tq)
        for hd, sl in enumerate(slices):
            s = s_scr[slot, hd]
            if masked:
                r = lax.broadcasted_iota(jnp.int32, s.shape, 0)
                cidx = lax.broadcasted_iota(jnp.int32, s.shape, 1)
                s = jnp.where(cidx <= r, s, NEG_BIG)
            m = m_scr[hd]
            m_new = jnp.maximum(m, jnp.max(s, axis=-1, keepdims=True))
            p = jnp.exp2(s - m_new)
            acc_scr[hd] = jnp.exp2(m - m_new) * acc_scr[hd] + jnp.dot(
                p.astype(BF16), v_ref[pl.ds(start, tq), sl], preferred_element_type=F32)
            m_scr[hd] = m_new

    m_scr[...] = jnp.full(m_scr.shape, NEG_BIG, F32)
    acc_scr[...] = jnp.zeros(acc_scr.shape, F32)
    scores(0, 0)

    def pair(jj, carry):
        j0 = 2 * jj
        scores(j0 + 1, 1)
        update(j0, 0, False)
        scores(j0 + 2, 0)
        update(j0 + 1, 1, False)
        return carry

    lax.fori_loop(0, qi // 2, pair, 0)

    @pl.when(qi % 2 == 1)
    def _():
        scores(qi, 1)
        update(qi - 1, 0, False)

    update(qi, qi % 2, True)
    for hd in range(heads_per_step):
        acc = acc_scr[hd]
        denom = acc[:, MLA_V:MLA_V + 1]
        o_ref[:, hd * MLA_V:(hd + 1) * MLA_V] = (acc[:, :MLA_V] / denom).astype(BF16)


def _mla_attn(q, k, v, batch, seq, *, tq=512, heads_per_step=2):
    width = MLA_HEADS * MLA_HEAD_LANES
    q3, k3, v3 = (a.reshape(batch, seq, width) for a in (q, k, v))
    hw = heads_per_step * MLA_HEAD_LANES
    out = pl.pallas_call(
        functools.partial(_mla_attn_kernel, tq=tq, heads_per_step=heads_per_step),
        grid=(batch, MLA_HEADS // heads_per_step, seq // tq),
        in_specs=[
            pl.BlockSpec((None, tq, hw), lambda b, hp, i: (b, i, hp)),
            pl.BlockSpec((None, seq, hw), lambda b, hp, i: (b, 0, hp)),
            pl.BlockSpec((None, seq, hw), lambda b, hp, i: (b, 0, hp)),
        ],
        out_specs=pl.BlockSpec((None, tq, heads_per_step * MLA_V), lambda b, hp, i: (b, i, hp)),
        out_shape=jax.ShapeDtypeStruct((batch, seq, MLA_WIDTH), BF16),
        scratch_shapes=[
            pltpu.VMEM((2, heads_per_step, tq, tq), F32),
            pltpu.VMEM((heads_per_step, tq, 1), F32),
            pltpu.VMEM((heads_per_step, tq, MLA_HEAD_LANES), F32),
        ],
        compiler_params=_params(3),
        name="mla_attn",
    )(q3, k3, v3)
    return out.reshape(batch * seq, MLA_WIDTH)


def _dil_attn_kernel(q_ref, kp_ref, kc_ref, vp_ref, vc_ref, o_ref, lse_ref,
                     kk_scr, vv_scr, *, nb):
    chunk = pl.program_id(2)
    blk = DIL_BLOCK
    kk_scr[:blk, :] = kp_ref[...]
    kk_scr[blk:, :] = kc_ref[...]
    vv_scr[:blk, :] = vp_ref[...]
    vv_scr[blk:, :] = vc_ref[...]

    r = lax.broadcasted_iota(jnp.int32, (blk, 2 * blk), 0)
    cidx = lax.broadcasted_iota(jnp.int32, (blk, 2 * blk), 1)
    dist = r + blk - cidx
    band = (dist >= 0) & (dist <= blk)
    lane = lax.broadcasted_iota(jnp.int32, (blk, LANES), 1)
    low = lane < DIL_DH

    for j in range(nb):
        rows = slice(j * blk, (j + 1) * blk)
        band_rows = slice(j * blk, (j + 2) * blk)
        if j == 0:
            mask = band & ((cidx >= blk) | (chunk > 0))
        else:
            mask = band
        for pair in range(DIL_HPG // 2):
            cols = slice(pair * LANES, (pair + 1) * LANES)
            q2 = q_ref[rows, cols]
            k2 = kk_scr[band_rows, cols]
            v2 = vv_scr[band_rows, cols]
            outs, lses = [], []
            for keep in (low, ~low):
                qh = jnp.where(keep, q2, jnp.zeros_like(q2))
                s = lax.dot_general(qh, k2, _NT, preferred_element_type=F32)
                s = jnp.where(mask, s, NEG_BIG)
                m = jnp.max(s, axis=-1, keepdims=True)
                p = jnp.exp(s - m)
                denom = jnp.sum(p, axis=-1, keepdims=True)
                outs.append(jnp.dot(p.astype(BF16), v2, preferred_element_type=F32) / denom)
                lses.append(m + jnp.log(denom))
            o_ref[rows, cols] = jnp.where(low, outs[0], outs[1]).astype(BF16)
            lse_ref[rows, cols] = jnp.where(low, lses[0], lses[1])


def _dil_attn(qkv, group):
    batch, d, sub_len, _ = qkv.shape
    nb = DIL_BLOCKS_PER_STEP[group]
    blk = DIL_BLOCK
    rows = nb * blk
    cur = lambda part: pl.BlockSpec((None, None, rows, DIL_WIDTH),
                                    lambda b, r, n: (b, r, n, part))
    prev = lambda part: pl.BlockSpec((None, None, blk, DIL_WIDTH),
                                     lambda b, r, n: (b, r, jnp.maximum(n * nb - 1, 0), part))
    out = pl.BlockSpec((None, None, rows, DIL_WIDTH), lambda b, r, n: (b, r, n, 0))
    return pl.pallas_call(
        functools.partial(_dil_attn_kernel, nb=nb),
        grid=(batch, d, sub_len // rows),
        in_specs=[cur(0), prev(1), cur(1), prev(2), cur(2)],
        out_specs=[out, out],
        out_shape=[
            jax.ShapeDtypeStruct((batch, d, sub_len, DIL_WIDTH), BF16),
            jax.ShapeDtypeStruct((batch, d, sub_len, DIL_WIDTH), F32),
        ],
        scratch_shapes=[pltpu.VMEM((rows + blk, DIL_WIDTH), BF16)] * 2,
        compiler_params=_params(3),
        name=f"dil_attn_{group}",
    )(qkv, qkv, qkv, qkv, qkv)


def _sigmoid(x):
    return 1.0 / (1.0 + jnp.exp(-x))


def _out_kernel(x_ref, ymla_ref, o0_ref, o1_ref, o2_ref, l0_ref, l1_ref, l2_ref,
                z_ref, gate_ref, wpm_ref, wpd_ref, wo_ref, pg_ref, out_ref, perm_scr, *, tm):
    n_sl = DIL_WIDTH // LANES

    def token_order(ref, d, slot):
        rows = tm // d
        for r in range(d):
            for s in range(n_sl):
                perm_scr[slot * n_sl + s, pl.ds(r, rows, stride=d), :] = (
                    ref[r, :, s * LANES:(s + 1) * LANES].astype(F32))
        return jnp.concatenate([perm_scr[slot * n_sl + s] for s in range(n_sl)], axis=1)

    d1, d2 = DIL_DILATIONS[1], DIL_DILATIONS[2]
    o0, l0 = o0_ref[0].astype(F32), l0_ref[0]
    o1, l1 = token_order(o1_ref, d1, 0), token_order(l1_ref, d1, 1)
    o2, l2 = token_order(o2_ref, d2, 2), token_order(l2_ref, d2, 3)
    m = jnp.maximum(jnp.maximum(l0, l1), l2)
    e0, e1, e2 = jnp.exp(l0 - m), jnp.exp(l1 - m), jnp.exp(l2 - m)
    y_dil = (e0 * o0 + e1 * o1 + e2 * o2) / (e0 + e1 + e2)

    z_mla = z_ref[:, :MLA_WIDTH].astype(F32)
    z_dil = z_ref[:, MLA_WIDTH:].astype(F32)
    a = (ymla_ref[...].astype(F32) * (z_mla * _sigmoid(z_mla))).astype(BF16)
    b = (y_dil * (z_dil * _sigmoid(z_dil))).astype(BF16)
    pa = jnp.dot(a, wpm_ref[...], preferred_element_type=F32)
    pb = jnp.dot(b, wpd_ref[...], preferred_element_type=F32)
    merged = (_sigmoid(gate_ref[:, :D_MODEL].astype(F32)) * pa
              + _sigmoid(gate_ref[:, D_MODEL:].astype(F32)) * pb)
    u = jnp.dot(merged.astype(BF16), wo_ref[...], preferred_element_type=F32)
    out_ref[...] = x_ref[...] + _rms(u, pg_ref[...])


def _out_block(x2, y_mla, o_dil, lse_dil, z, gate, wpm, wpd, wo, pg, seq, *, tm=512):
    t = x2.shape[0]
    tiles_per_seq = seq // tm
    row = lambda i: (i, 0)
    const = lambda i: (0, 0)
    rows = lambda w: pl.BlockSpec((tm, w), row)
    sub = lambda d: pl.BlockSpec((None, d, tm // d, DIL_WIDTH),
                                 lambda i: (i // tiles_per_seq, 0, i % tiles_per_seq, 0))
    subs = [sub(d) for d in DIL_DILATIONS]
    return pl.pallas_call(
        functools.partial(_out_kernel, tm=tm),
        grid=(t // tm,),
        in_specs=[rows(D_MODEL), rows(MLA_WIDTH)] + subs + subs
        + [rows(Z_WIDTH), rows(G_WIDTH),
           pl.BlockSpec((MLA_WIDTH, D_MODEL), const),
           pl.BlockSpec((DIL_WIDTH, D_MODEL), const),
           pl.BlockSpec((D_MODEL, D_MODEL), const),
           pl.BlockSpec((1, D_MODEL), const)],
        out_specs=rows(D_MODEL),
        out_shape=jax.ShapeDtypeStruct((t, D_MODEL), F32),
        scratch_shapes=[pltpu.VMEM((4 * DIL_WIDTH // LANES, tm, LANES), F32)],
        compiler_params=_params(1),
        name="out_block",
    )(x2, y_mla, *o_dil, *lse_dil, z, gate, wpm, wpd, wo, pg)


def _inv_freq(rot_dim):
    return ROPE_THETA ** (-jnp.arange(0, rot_dim, 2, dtype=F32) / rot_dim)


def _lane_freqs():
    f = jnp.zeros((LANES,), F32)
    f = f.at[:DIL_ROPE].set(jnp.tile(_inv_freq(DIL_ROPE), 2))
    f = f.at[MLA_NOPE:MLA_NOPE + MLA_ROPE].set(jnp.tile(_inv_freq(MLA_ROPE), 2))
    return f[None, :]


def _layer_weights(w_in, w_uq, w_ukv):
    offs = np.cumsum([0, MLA_Q_RANK, MLA_KV_RANK, MLA_ROPE, DIL_GROUPS * QKV_WIDTH, MLA_WIDTH,
                      DIL_WIDTH, D_MODEL, D_MODEL])
    w_in = w_in.astype(BF16)
    zeros = lambda n: jnp.zeros((D_MODEL, n), BF16)
    qkv0 = int(offs[3])
    qkv = [w_in[:, qkv0 + (part * DIL_GROUPS + g) * DIL_WIDTH:
                qkv0 + (part * DIL_GROUPS + g + 1) * DIL_WIDTH]
           for g in range(DIL_GROUPS) for part in range(3)]
    w_cat = jnp.concatenate(
        [w_in[:, :offs[2]], zeros(MLA_NOPE), w_in[:, offs[2]:offs[3]],
         zeros(LANES - MLA_NOPE - MLA_ROPE)] + qkv + [w_in[:, offs[4]:]], axis=1)

    pad_heads = lambda w, n: jnp.pad(w, ((0, 0), (0, 0), (0, MLA_HEAD_LANES - n)))
    wq = pad_heads(w_uq.reshape(MLA_Q_RANK, MLA_HEADS, MLA_NOPE + MLA_ROPE), MLA_NOPE + MLA_ROPE)
    wkv = w_ukv.reshape(MLA_KV_RANK, MLA_HEADS, MLA_NOPE + MLA_V)
    wk = pad_heads(wkv[:, :, :MLA_NOPE], MLA_NOPE)
    wv = pad_heads(wkv[:, :, MLA_NOPE:], MLA_V)
    flat = lambda w: w.reshape(w.shape[0], MLA_HEADS * MLA_HEAD_LANES).astype(BF16)
    return w_cat, flat(wq), flat(wk), flat(wv)


def kernel(x, positions, pre_norm_g, w_in, q_norm_g, w_uq, kv_norm_g, w_ukv,
           w_proj_mla, w_proj_dil, w_out, post_norm_g):
    batch, seq, _ = x.shape
    depth = w_in.shape[0]
    pos_col = positions.reshape(batch * seq, 1)
    freqs = _lane_freqs()
    x2 = x.reshape(batch * seq, D_MODEL)
    for layer in range(depth):
        w_cat, wq, wk, wv = _layer_weights(w_in[layer], w_uq[layer], w_ukv[layer])
        q, k, v, qkv0, qkv1, qkv2, z, gate = _in_proj(
            x2, pos_col, pre_norm_g[layer][None, :], w_cat, freqs,
            q_norm_g[layer][None, :], kv_norm_g[layer][None, :], wq, wk, wv, batch, seq)
        y_mla = _mla_attn(q, k, v, batch, seq)
        dil = [_dil_attn(qkv, g) for g, qkv in enumerate((qkv0, qkv1, qkv2))]
        x2 = _out_block(x2, y_mla, [o for o, _ in dil], [l for _, l in dil], z, gate,
                        w_proj_mla[layer].astype(BF16), w_proj_dil[layer].astype(BF16),
                        w_out[layer].astype(BF16), post_norm_g[layer][None, :], seq)
    return x2.reshape(batch, seq, D_MODEL)
```

```python
import functools
import math

import jax
import jax.numpy as jnp
import numpy as np
from jax import lax
from jax.experimental import pallas as pl
from jax.experimental.pallas import tpu as pltpu

F32 = jnp.float32
BF16 = jnp.bfloat16

D_MODEL = 1024
ROPE_THETA = 500000.0
NORM_EPS = 1e-6

MLA_HEADS = 8
MLA_Q_RANK = 384
MLA_KV_RANK = 256
MLA_NOPE = 64
MLA_ROPE = 32
MLA_V = 64
MLA_WIDTH = MLA_HEADS * MLA_V
MLA_HEAD_LANES = 128
MLA_ROW_CHUNK = 64

DIL_DILATIONS = (1, 4, 16)
DIL_GROUPS = 3
DIL_HPG = 8
DIL_DH = 64
DIL_ROPE = 16
DIL_WIDTH = DIL_HPG * DIL_DH
DIL_BLOCK = 128
DIL_BLOCKS_PER_STEP = (4, 4, 2)

LANES = 128
NEG_BIG = -1e30
LOG2E = math.log2(math.e)
LN2 = math.log(2.0)

VMEM_LIMIT = 56 * 1024 * 1024

LAT_WIDTH = MLA_Q_RANK + MLA_KV_RANK + LANES
QKV_WIDTH = 3 * DIL_WIDTH
Z_WIDTH = MLA_WIDTH + DIL_WIDTH
G_WIDTH = 2 * D_MODEL
IN_TOTAL = LAT_WIDTH + DIL_GROUPS * QKV_WIDTH + Z_WIDTH + G_WIDTH
N_SLABS = D_MODEL // LANES


def _params(n_axes):
    return pltpu.CompilerParams(
        dimension_semantics=("arbitrary",) * n_axes, vmem_limit_bytes=VMEM_LIMIT)


def _trig_tables(pos_col, freq_row):
    ang = pos_col.astype(F32) * freq_row
    return jnp.cos(ang), jnp.sin(ang)


def _rope_tables(cos, sin, lane_lo, width, period):
    lane = lax.broadcasted_iota(jnp.int32, cos.shape, 1) % period
    in_a = (lane >= lane_lo) & (lane < lane_lo + width)
    in_b = (lane >= lane_lo + width) & (lane < lane_lo + 2 * width)
    c = jnp.where(in_a | in_b, cos, 1.0)
    return c, jnp.where(in_a, -sin, 0.0), jnp.where(in_b, sin, 0.0)


def _apply_rope(a, c, sa, sb, shift):
    return a * c + pltpu.roll(a, LANES - shift, 1) * sa + pltpu.roll(a, shift, 1) * sb


def _rms(x, g):
    ms = jnp.mean(x * x, axis=-1, keepdims=True)
    return x * lax.rsqrt(ms + NORM_EPS) * g


def _sigmoid(x):
    return 1.0 / (1.0 + jnp.exp(-x))


def _in_proj_kernel(x_ref, pos_ref, g_ref, w_ref, freq_ref, qg_ref, kvg_ref,
                    wq_ref, wk_ref, wv_ref,
                    q_ref, k_ref, v_ref, qkv0_ref, qkv1_ref, qkv2_ref, z_ref, gate_ref,
                    h_scr, tab_scr, hp_scr, *, tm):
    h = _rms(x_ref[...], g_ref[...])
    hb = h.astype(BF16)
    cos, sin = _trig_tables(pos_ref[...], freq_ref[...])

    def proj(lhs, c0, width):
        return jnp.dot(lhs, w_ref[:, c0:c0 + width], preferred_element_type=F32)

    lat = proj(hb, 0, LAT_WIDTH)
    cq = _rms(lat[:, :MLA_Q_RANK], qg_ref[...]).astype(BF16)
    ckv = _rms(lat[:, MLA_Q_RANK:MLA_Q_RANK + MLA_KV_RANK], kvg_ref[...]).astype(BF16)
    mhalf = MLA_ROPE // 2
    mtabs = _rope_tables(cos, sin, MLA_NOPE, mhalf, LANES)
    kr = _apply_rope(lat[:, MLA_Q_RANK + MLA_KV_RANK:], *mtabs, mhalf)
    q_scale = (MLA_NOPE + MLA_ROPE) ** -0.5 * LOG2E
    lane = lax.broadcasted_iota(jnp.int32, kr.shape, 1)
    ones_col = jnp.where(lane == MLA_V, 1.0, 0.0)
    q_all = jnp.dot(cq, wq_ref[...], preferred_element_type=F32)
    k_all = jnp.dot(ckv, wk_ref[...], preferred_element_type=F32)
    v_all = jnp.dot(ckv, wv_ref[...], preferred_element_type=F32)
    for hd in range(MLA_HEADS):
        sl = slice(hd * MLA_HEAD_LANES, (hd + 1) * MLA_HEAD_LANES)
        q_ref[:, sl] = (_apply_rope(q_all[:, sl], *mtabs, mhalf) * q_scale).astype(BF16)
        k_ref[:, sl] = (k_all[:, sl] + kr).astype(BF16)
        v_ref[:, sl] = (v_all[:, sl] + ones_col).astype(BF16)

    dhalf = DIL_ROPE // 2
    first = lax.broadcasted_iota(jnp.int32, cos.shape, 1) < DIL_DH
    dtabs = _rope_tables(jnp.where(first, cos, pltpu.roll(cos, DIL_DH, 1)),
                         jnp.where(first, sin, pltpu.roll(sin, DIL_DH, 1)), 0, dhalf, DIL_DH)
    d_scale = DIL_DH ** -0.5 * LOG2E

    def qkv_pieces(lhs, c0, tables):
        for part in range(3):
            acc = proj(lhs, c0 + part * DIL_WIDTH, DIL_WIDTH)
            for l0 in range(0, DIL_WIDTH, LANES):
                a = acc[:, l0:l0 + LANES]
                if part < 2:
                    a = _apply_rope(a, *tables, dhalf)
                if part == 0:
                    a = a * d_scale
                yield part * DIL_WIDTH + l0, a.astype(BF16)

    col = LAT_WIDTH
    for off, piece in qkv_pieces(hb, col, dtabs):
        qkv0_ref[0, :, off:off + LANES] = piece
    col += QKV_WIDTH

    for s in range(N_SLABS):
        h_scr[s] = h[:, s * LANES:(s + 1) * LANES]
    for i, t in enumerate(dtabs):
        tab_scr[i] = t
    for d, out_ref in ((DIL_DILATIONS[1], qkv1_ref), (DIL_DILATIONS[2], qkv2_ref)):
        rows = tm // d
        for r in range(d):
            for s in range(N_SLABS):
                hp_scr[r * rows:(r + 1) * rows, s * LANES:(s + 1) * LANES] = (
                    h_scr[s, pl.ds(r, rows, stride=d), :].astype(BF16))
        ptabs = tuple(
            jnp.concatenate([tab_scr[i, pl.ds(r, rows, stride=d), :] for r in range(d)], axis=0)
            for i in range(3))
        for off, piece in qkv_pieces(hp_scr[...], col, ptabs):
            for r in range(d):
                out_ref[r, :, off:off + LANES] = piece[r * rows:(r + 1) * rows, :]
        col += QKV_WIDTH

    for c0 in range(0, Z_WIDTH, 512):
        zacc = proj(hb, col + c0, 512)
        z_ref[:, c0:c0 + 512] = (zacc * _sigmoid(zacc)).astype(BF16)
    col += Z_WIDTH
    for c0 in range(0, G_WIDTH, 512):
        gate_ref[:, c0:c0 + 512] = _sigmoid(proj(hb, col + c0, 512)).astype(BF16)


def _in_proj(x2, pos_col, g_pre, w_cat, freqs, qg, kvg, wq, wk, wv, batch, seq, *, tm=512):
    t = x2.shape[0]
    tiles_per_seq = seq // tm
    mla_width = MLA_HEADS * MLA_HEAD_LANES
    row = lambda i: (i, 0)
    const = lambda i: (0, 0)
    resident = lambda shape: pl.BlockSpec(shape, const, pipeline_mode=pl.Buffered(1))

    def qkv_spec(d):
        return pl.BlockSpec((None, d, tm // d, QKV_WIDTH),
                            lambda i: (i // tiles_per_seq, 0, i % tiles_per_seq, 0))

    return pl.pallas_call(
        functools.partial(_in_proj_kernel, tm=tm),
        grid=(t // tm,),
        in_specs=[
            pl.BlockSpec((tm, D_MODEL), row),
            pl.BlockSpec((tm, 1), row),
            resident((1, D_MODEL)),
            resident((D_MODEL, IN_TOTAL)),
            resident((1, LANES)),
            resident((1, MLA_Q_RANK)),
            resident((1, MLA_KV_RANK)),
            resident((MLA_Q_RANK, mla_width)),
            resident((MLA_KV_RANK, mla_width)),
            resident((MLA_KV_RANK, mla_width)),
        ],
        out_specs=[pl.BlockSpec((tm, mla_width), row)] * 3
        + [qkv_spec(d) for d in DIL_DILATIONS]
        + [pl.BlockSpec((tm, Z_WIDTH), row), pl.BlockSpec((tm, G_WIDTH), row)],
        out_shape=[jax.ShapeDtypeStruct((t, mla_width), BF16)] * 3
        + [jax.ShapeDtypeStruct((batch, d, seq // d, QKV_WIDTH), BF16) for d in DIL_DILATIONS]
        + [jax.ShapeDtypeStruct((t, Z_WIDTH), BF16), jax.ShapeDtypeStruct((t, G_WIDTH), BF16)],
        scratch_shapes=[
            pltpu.VMEM((N_SLABS, tm, LANES), F32),
            pltpu.VMEM((3, tm, LANES), F32),
            pltpu.VMEM((tm, D_MODEL), BF16),
        ],
        compiler_params=_params(1),
        name="in_proj",
    )(x2, pos_col, g_pre, w_cat, freqs, qg, kvg, wq, wk, wv)


_NT = (((1,), (1,)), ((), ()))


def _mla_attn_kernel(q_ref, k_ref, v_ref, o_ref, s_scr, p_scr, m_scr, alpha_scr, acc_scr, *,
                     tk, heads_per_step):
    qi = pl.program_id(2)
    tq = 2 * tk
    slices = [slice(hd * MLA_HEAD_LANES, (hd + 1) * MLA_HEAD_LANES)
              for hd in range(heads_per_step)]
    every = slice(0, tq)
    upper, lower = slice(0, tk), slice(tk, tq)

    def scores(j, slot, rows=every):
        start = pl.multiple_of(j * tk, tk)
        for hd, sl in enumerate(slices):
            s_scr[slot, hd, rows, :] = lax.dot_general(
                q_ref[rows, sl], k_ref[pl.ds(start, tk), sl], _NT, preferred_element_type=F32)

    def update(j, slot, rows=every, masked=False):
        start = pl.multiple_of(j * tk, tk)
        for hd, sl in enumerate(slices):
            for c0 in range(rows.start, rows.stop, MLA_ROW_CHUNK):
                rc = slice(c0, c0 + MLA_ROW_CHUNK)
                s = s_scr[slot, hd, rc, :]
                if masked:
                    r = lax.broadcasted_iota(jnp.int32, s.shape, 0) + (c0 - rows.start)
                    cidx = lax.broadcasted_iota(jnp.int32, s.shape, 1)
                    s = jnp.where(cidx <= r, s, NEG_BIG)
                m = m_scr[hd, rc, :]
                m_new = jnp.maximum(m, jnp.max(s, axis=-1, keepdims=True))
                p_scr[hd, rc, :] = jnp.exp2(s - m_new).astype(BF16)
                alpha_scr[hd, rc, :] = jnp.exp2(m - m_new)
                m_scr[hd, rc, :] = m_new
            acc_scr[hd, rows, :] = alpha_scr[hd, rows, :] * acc_scr[hd, rows, :] + jnp.dot(
                p_scr[hd, rows, :], v_ref[pl.ds(start, tk), sl], preferred_element_type=F32)

    m_scr[...] = jnp.full(m_scr.shape, NEG_BIG, F32)
    acc_scr[...] = jnp.zeros(acc_scr.shape, F32)
    scores(0, 0)

    def pair(jj, carry):
        j0 = 2 * jj
        scores(j0 + 1, 1)
        update(j0, 0)
        scores(j0 + 2, 0)
        update(j0 + 1, 1)
        return carry

    lax.fori_loop(0, qi, pair, 0)

    jd = 2 * qi
    scores(jd + 1, 1, lower)
    update(jd, 0, upper, masked=True)
    update(jd, 0, lower)
    update(jd + 1, 1, lower, masked=True)
    for hd in range(heads_per_step):
        acc = acc_scr[hd]
        denom = acc[:, MLA_V:MLA_V + 1]
        o_ref[:, hd * MLA_V:(hd + 1) * MLA_V] = (acc[:, :MLA_V] / denom).astype(BF16)


def _mla_attn(q, k, v, batch, seq, *, tk=512, heads_per_step=2):
    width = MLA_HEADS * MLA_HEAD_LANES
    tq = 2 * tk
    q3, k3, v3 = (a.reshape(batch, seq, width) for a in (q, k, v))
    hw = heads_per_step * MLA_HEAD_LANES
    out = pl.pallas_call(
        functools.partial(_mla_attn_kernel, tk=tk, heads_per_step=heads_per_step),
        grid=(batch, MLA_HEADS // heads_per_step, seq // tq),
        in_specs=[
            pl.BlockSpec((None, tq, hw), lambda b, hp, i: (b, i, hp)),
            pl.BlockSpec((None, seq, hw), lambda b, hp, i: (b, 0, hp)),
            pl.BlockSpec((None, seq, hw), lambda b, hp, i: (b, 0, hp)),
        ],
        out_specs=pl.BlockSpec((None, tq, heads_per_step * MLA_V), lambda b, hp, i: (b, i, hp)),
        out_shape=jax.ShapeDtypeStruct((batch, seq, MLA_WIDTH), BF16),
        scratch_shapes=[
            pltpu.VMEM((2, heads_per_step, tq, tk), F32),
            pltpu.VMEM((heads_per_step, tq, tk), BF16),
            pltpu.VMEM((heads_per_step, tq, 1), F32),
            pltpu.VMEM((heads_per_step, tq, 1), F32),
            pltpu.VMEM((heads_per_step, tq, MLA_HEAD_LANES), F32),
        ],
        compiler_params=_params(3),
        name="mla_attn",
    )(q3, k3, v3)
    return out.reshape(batch * seq, MLA_WIDTH)


def _dil_attn_kernel(q_ref, kp_ref, kc_ref, vp_ref, vc_ref, o_ref, lse_ref,
                     kk_scr, vv_scr, *, nb):
    chunk = pl.program_id(2)
    blk = DIL_BLOCK
    kk_scr[:blk, :] = kp_ref[...]
    kk_scr[blk:, :] = kc_ref[...]
    vv_scr[:blk, :] = vp_ref[...]
    vv_scr[blk:, :] = vc_ref[...]

    r = lax.broadcasted_iota(jnp.int32, (blk, 2 * blk), 0)
    cidx = lax.broadcasted_iota(jnp.int32, (blk, 2 * blk), 1)
    dist = r + blk - cidx
    band = (dist >= 0) & (dist <= blk)
    lane = lax.broadcasted_iota(jnp.int32, (blk, LANES), 1)
    low = lane < DIL_DH

    for j in range(nb):
        rows = slice(j * blk, (j + 1) * blk)
        band_rows = slice(j * blk, (j + 2) * blk)
        if j == 0:
            mask = band & ((cidx >= blk) | (chunk > 0))
        else:
            mask = band
        for pair in range(DIL_HPG // 2):
            cols = slice(pair * LANES, (pair + 1) * LANES)
            q2 = q_ref[rows, cols]
            k2 = kk_scr[band_rows, cols]
            v2 = vv_scr[band_rows, cols]
            outs, lses = [], []
            for keep in (low, ~low):
                qh = jnp.where(keep, q2, jnp.zeros_like(q2))
                s = lax.dot_general(qh, k2, _NT, preferred_element_type=F32)
                s = jnp.where(mask, s, NEG_BIG)
                m = jnp.max(s, axis=-1, keepdims=True)
                p = jnp.exp2(s - m)
                denom = jnp.sum(p, axis=-1, keepdims=True)
                outs.append(jnp.dot(p.astype(BF16), v2, preferred_element_type=F32) / denom)
                lses.append((m + jnp.log2(denom)) * LN2)
            o_ref[rows, cols] = jnp.where(low, outs[0], outs[1]).astype(BF16)
            lse_ref[rows, cols] = jnp.where(low, lses[0], lses[1])


def _dil_attn(qkv, group):
    batch, d, sub_len, _ = qkv.shape
    nb = DIL_BLOCKS_PER_STEP[group]
    blk = DIL_BLOCK
    rows = nb * blk
    cur = lambda part: pl.BlockSpec((None, None, rows, DIL_WIDTH),
                                    lambda b, r, n: (b, r, n, part))
    prev = lambda part: pl.BlockSpec((None, None, blk, DIL_WIDTH),
                                     lambda b, r, n: (b, r, jnp.maximum(n * nb - 1, 0), part))
    out = pl.BlockSpec((None, None, rows, DIL_WIDTH), lambda b, r, n: (b, r, n, 0))
    return pl.pallas_call(
        functools.partial(_dil_attn_kernel, nb=nb),
        grid=(batch, d, sub_len // rows),
        in_specs=[cur(0), prev(1), cur(1), prev(2), cur(2)],
        out_specs=[out, out],
        out_shape=[
            jax.ShapeDtypeStruct((batch, d, sub_len, DIL_WIDTH), BF16),
            jax.ShapeDtypeStruct((batch, d, sub_len, DIL_WIDTH), F32),
        ],
        scratch_shapes=[pltpu.VMEM((rows + blk, DIL_WIDTH), BF16)] * 2,
        compiler_params=_params(3),
        name=f"dil_attn_{group}",
    )(qkv, qkv, qkv, qkv, qkv)


def _out_kernel(x_ref, ymla_ref, o0_ref, o1_ref, o2_ref, l0_ref, l1_ref, l2_ref,
                z_ref, gate_ref, wpm_ref, wpd_ref, wo_ref, pg_ref, out_ref, perm_scr, *, tm):
    n_sl = DIL_WIDTH // LANES

    def token_order(ref, d, slot):
        rows = tm // d
        for r in range(d):
            for s in range(n_sl):
                perm_scr[slot * n_sl + s, pl.ds(r, rows, stride=d), :] = (
                    ref[r, :, s * LANES:(s + 1) * LANES].astype(F32))
        return jnp.concatenate([perm_scr[slot * n_sl + s] for s in range(n_sl)], axis=1)

    d1, d2 = DIL_DILATIONS[1], DIL_DILATIONS[2]
    o0, l0 = o0_ref[0].astype(F32), l0_ref[0]
    o1, l1 = token_order(o1_ref, d1, 0), token_order(l1_ref, d1, 1)
    o2, l2 = token_order(o2_ref, d2, 2), token_order(l2_ref, d2, 3)
    m = jnp.maximum(jnp.maximum(l0, l1), l2)
    e0, e1, e2 = jnp.exp(l0 - m), jnp.exp(l1 - m), jnp.exp(l2 - m)
    y_dil = (e0 * o0 + e1 * o1 + e2 * o2) / (e0 + e1 + e2)

    a = (ymla_ref[...].astype(F32) * z_ref[:, :MLA_WIDTH].astype(F32)).astype(BF16)
    b = (y_dil * z_ref[:, MLA_WIDTH:].astype(F32)).astype(BF16)
    pa = jnp.dot(a, wpm_ref[...], preferred_element_type=F32)
    pb = jnp.dot(b, wpd_ref[...], preferred_element_type=F32)
    merged = (gate_ref[:, :D_MODEL].astype(F32) * pa
              + gate_ref[:, D_MODEL:].astype(F32) * pb)
    u = jnp.dot(merged.astype(BF16), wo_ref[...], preferred_element_type=F32)
    out_ref[...] = x_ref[...] + _rms(u, pg_ref[...])


def _out_block(x2, y_mla, o_dil, lse_dil, z, gate, wpm, wpd, wo, pg, seq, *, tm=512):
    t = x2.shape[0]
    tiles_per_seq = seq // tm
    row = lambda i: (i, 0)
    const = lambda i: (0, 0)
    rows = lambda w: pl.BlockSpec((tm, w), row)
    sub = lambda d: pl.BlockSpec((None, d, tm // d, DIL_WIDTH),
                                 lambda i: (i // tiles_per_seq, 0, i % tiles_per_seq, 0))
    subs = [sub(d) for d in DIL_DILATIONS]
    return pl.pallas_call(
        functools.partial(_out_kernel, tm=tm),
        grid=(t // tm,),
        in_specs=[rows(D_MODEL), rows(MLA_WIDTH)] + subs + subs
        + [rows(Z_WIDTH), rows(G_WIDTH),
           pl.BlockSpec((MLA_WIDTH, D_MODEL), const),
           pl.BlockSpec((DIL_WIDTH, D_MODEL), const),
           pl.BlockSpec((D_MODEL, D_MODEL), const),
           pl.BlockSpec((1, D_MODEL), const)],
        out_specs=rows(D_MODEL),
        out_shape=jax.ShapeDtypeStruct((t, D_MODEL), F32),
        scratch_shapes=[pltpu.VMEM((4 * DIL_WIDTH // LANES, tm, LANES), F32)],
        compiler_params=_params(1),
        name="out_block",
    )(x2, y_mla, *o_dil, *lse_dil, z, gate, wpm, wpd, wo, pg)


def _inv_freq(rot_dim):
    return ROPE_THETA ** (-jnp.arange(0, rot_dim, 2, dtype=F32) / rot_dim)


def _lane_freqs():
    f = jnp.zeros((LANES,), F32)
    f = f.at[:DIL_ROPE].set(jnp.tile(_inv_freq(DIL_ROPE), 2))
    f = f.at[MLA_NOPE:MLA_NOPE + MLA_ROPE].set(jnp.tile(_inv_freq(MLA_ROPE), 2))
    return f[None, :]


def _layer_weights(w_in, w_uq, w_ukv):
    offs = np.cumsum([0, MLA_Q_RANK, MLA_KV_RANK, MLA_ROPE, DIL_GROUPS * QKV_WIDTH, MLA_WIDTH,
                      DIL_WIDTH, D_MODEL, D_MODEL])
    w_in = w_in.astype(BF16)
    zeros = lambda n: jnp.zeros((D_MODEL, n), BF16)
    qkv0 = int(offs[3])
    qkv = [w_in[:, qkv0 + (part * DIL_GROUPS + g) * DIL_WIDTH:
                qkv0 + (part * DIL_GROUPS + g + 1) * DIL_WIDTH]
           for g in range(DIL_GROUPS) for part in range(3)]
    w_cat = jnp.concatenate(
        [w_in[:, :offs[2]], zeros(MLA_NOPE), w_in[:, offs[2]:offs[3]],
         zeros(LANES - MLA_NOPE - MLA_ROPE)] + qkv + [w_in[:, offs[4]:]], axis=1)

    pad_heads = lambda w, n: jnp.pad(w, ((0, 0), (0, 0), (0, MLA_HEAD_LANES - n)))
    wq = pad_heads(w_uq.reshape(MLA_Q_RANK, MLA_HEADS, MLA_NOPE + MLA_ROPE), MLA_NOPE + MLA_ROPE)
    wkv = w_ukv.reshape(MLA_KV_RANK, MLA_HEADS, MLA_NOPE + MLA_V)
    wk = pad_heads(wkv[:, :, :MLA_NOPE], MLA_NOPE)
    wv = pad_heads(wkv[:, :, MLA_NOPE:], MLA_V)
    flat = lambda w: w.reshape(w.shape[0], MLA_HEADS * MLA_HEAD_LANES).astype(BF16)
    return w_cat, flat(wq), flat(wk), flat(wv)


def kernel(x, positions, pre_norm_g, w_in, q_norm_g, w_uq, kv_norm_g, w_ukv,
           w_proj_mla, w_proj_dil, w_out, post_norm_g):
    batch, seq, _ = x.shape
    depth = w_in.shape[0]
    pos_col = positions.reshape(batch * seq, 1)
    freqs = _lane_freqs()
    x2 = x.reshape(batch * seq, D_MODEL)
    for layer in range(depth):
        w_cat, wq, wk, wv = _layer_weights(w_in[layer], w_uq[layer], w_ukv[layer])
        q, k, v, qkv0, qkv1, qkv2, z, gate = _in_proj(
            x2, pos_col, pre_norm_g[layer][None, :], w_cat, freqs,
            q_norm_g[layer][None, :], kv_norm_g[layer][None, :], wq, wk, wv, batch, seq)
        y_mla = _mla_attn(q, k, v, batch, seq)
        dil = [_dil_attn(qkv, g) for g, qkv in enumerate((qkv0, qkv1, qkv2))]
        x2 = _out_block(x2, y_mla, [o for o, _ in dil], [l for _, l in dil], z, gate,
                        w_proj_mla[layer].astype(BF16), w_proj_dil[layer].astype(BF16),
                        w_out[layer].astype(BF16), post_norm_g[layer][None, :], seq)
    return x2.reshape(batch, seq, D_MODEL)
```

```python
import functools
import math

import jax
import jax.numpy as jnp
import numpy as np
from jax import lax
from jax.experimental import pallas as pl
from jax.experimental.pallas import tpu as pltpu

F32 = jnp.float32
BF16 = jnp.bfloat16

D_MODEL = 1024
ROPE_THETA = 500000.0
NORM_EPS = 1e-6

MLA_HEADS = 8
MLA_Q_RANK = 384
MLA_KV_RANK = 256
MLA_NOPE = 64
MLA_ROPE = 32
MLA_V = 64
MLA_WIDTH = MLA_HEADS * MLA_V
MLA_HEAD_LANES = 128
MLA_ROW_CHUNK = 64

DIL_DILATIONS = (1, 4, 16)
DIL_GROUPS = 3
DIL_HPG = 8
DIL_DH = 64
DIL_ROPE = 16
DIL_WIDTH = DIL_HPG * DIL_DH
DIL_BLOCK = 128
DIL_BLOCKS_PER_STEP = (4, 4, 2)

LANES = 128
NEG_BIG = -1e30
LOG2E = math.log2(math.e)
LN2 = math.log(2.0)

VMEM_LIMIT = 56 * 1024 * 1024

LAT_WIDTH = MLA_Q_RANK + MLA_KV_RANK + LANES
QKV_WIDTH = 3 * DIL_WIDTH
Z_WIDTH = MLA_WIDTH + DIL_WIDTH
G_WIDTH = 2 * D_MODEL
IN_TOTAL = LAT_WIDTH + DIL_GROUPS * QKV_WIDTH + Z_WIDTH + G_WIDTH
N_SLABS = D_MODEL // LANES


def _params(n_axes):
    return pltpu.CompilerParams(
        dimension_semantics=("arbitrary",) * n_axes, vmem_limit_bytes=VMEM_LIMIT)


def _trig_tables(pos_col, freq_row):
    ang = pos_col.astype(F32) * freq_row
    return jnp.cos(ang), jnp.sin(ang)


def _rope_tables(cos, sin, lane_lo, width, period):
    lane = lax.broadcasted_iota(jnp.int32, cos.shape, 1) % period
    in_a = (lane >= lane_lo) & (lane < lane_lo + width)
    in_b = (lane >= lane_lo + width) & (lane < lane_lo + 2 * width)
    c = jnp.where(in_a | in_b, cos, 1.0)
    return c, jnp.where(in_a, -sin, 0.0), jnp.where(in_b, sin, 0.0)


def _apply_rope(a, c, sa, sb, shift):
    return a * c + pltpu.roll(a, LANES - shift, 1) * sa + pltpu.roll(a, shift, 1) * sb


def _rms(x, g):
    ms = jnp.mean(x * x, axis=-1, keepdims=True)
    return x * lax.rsqrt(ms + NORM_EPS) * g


def _sigmoid(x):
    return 1.0 / (1.0 + jnp.exp(-x))


def _in_proj_kernel(x_ref, pos_ref, g_ref, w_ref, freq_ref, qg_ref, kvg_ref,
                    wq_ref, wk_ref, wv_ref,
                    q_ref, k_ref, v_ref, qkv0_ref, qkv1_ref, qkv2_ref, z_ref, gate_ref,
                    h_scr, tab_scr, hp_scr, *, tm):
    h = _rms(x_ref[...], g_ref[...])
    hb = h.astype(BF16)
    cos, sin = _trig_tables(pos_ref[...], freq_ref[...])

    def proj(lhs, c0, width):
        return jnp.dot(lhs, w_ref[:, c0:c0 + width], preferred_element_type=F32)

    lat = proj(hb, 0, LAT_WIDTH)
    cq = _rms(lat[:, :MLA_Q_RANK], qg_ref[...]).astype(BF16)
    ckv = _rms(lat[:, MLA_Q_RANK:MLA_Q_RANK + MLA_KV_RANK], kvg_ref[...]).astype(BF16)
    mhalf = MLA_ROPE // 2
    mtabs = _rope_tables(cos, sin, MLA_NOPE, mhalf, LANES)
    kr = _apply_rope(lat[:, MLA_Q_RANK + MLA_KV_RANK:], *mtabs, mhalf)
    q_scale = (MLA_NOPE + MLA_ROPE) ** -0.5 * LOG2E
    lane = lax.broadcasted_iota(jnp.int32, kr.shape, 1)
    ones_col = jnp.where(lane == MLA_V, 1.0, 0.0)
    q_all = jnp.dot(cq, wq_ref[...], preferred_element_type=F32)
    k_all = jnp.dot(ckv, wk_ref[...], preferred_element_type=F32)
    v_all = jnp.dot(ckv, wv_ref[...], preferred_element_type=F32)
    for hd in range(MLA_HEADS):
        sl = slice(hd * MLA_HEAD_LANES, (hd + 1) * MLA_HEAD_LANES)
        q_ref[:, sl] = (_apply_rope(q_all[:, sl], *mtabs, mhalf) * q_scale).astype(BF16)
        k_ref[:, sl] = (k_all[:, sl] + kr).astype(BF16)
        v_ref[:, sl] = (v_all[:, sl] + ones_col).astype(BF16)

    dhalf = DIL_ROPE // 2
    first = lax.broadcasted_iota(jnp.int32, cos.shape, 1) < DIL_DH
    dtabs = _rope_tables(jnp.where(first, cos, pltpu.roll(cos, DIL_DH, 1)),
                         jnp.where(first, sin, pltpu.roll(sin, DIL_DH, 1)), 0, dhalf, DIL_DH)
    d_scale = DIL_DH ** -0.5 * LOG2E

    def qkv_pieces(lhs, c0, tables):
        for part in range(3):
            acc = proj(lhs, c0 + part * DIL_WIDTH, DIL_WIDTH)
            for l0 in range(0, DIL_WIDTH, LANES):
                a = acc[:, l0:l0 + LANES]
                if part < 2:
                    a = _apply_rope(a, *tables, dhalf)
                if part == 0:
                    a = a * d_scale
                yield part * DIL_WIDTH + l0, a.astype(BF16)

    col = LAT_WIDTH
    for off, piece in qkv_pieces(hb, col, dtabs):
        qkv0_ref[0, :, off:off + LANES] = piece
    col += QKV_WIDTH

    for s in range(N_SLABS):
        h_scr[s] = h[:, s * LANES:(s + 1) * LANES]
    for i, t in enumerate(dtabs):
        tab_scr[i] = t
    for d, out_ref in ((DIL_DILATIONS[1], qkv1_ref), (DIL_DILATIONS[2], qkv2_ref)):
        rows = tm // d
        for r in range(d):
            for s in range(N_SLABS):
                hp_scr[r * rows:(r + 1) * rows, s * LANES:(s + 1) * LANES] = (
                    h_scr[s, pl.ds(r, rows, stride=d), :].astype(BF16))
        ptabs = tuple(
            jnp.concatenate([tab_scr[i, pl.ds(r, rows, stride=d), :] for r in range(d)], axis=0)
            for i in range(3))
        for off, piece in qkv_pieces(hp_scr[...], col, ptabs):
            for r in range(d):
                out_ref[r, :, off:off + LANES] = piece[r * rows:(r + 1) * rows, :]
        col += QKV_WIDTH

    for c0 in range(0, Z_WIDTH, 512):
        zacc = proj(hb, col + c0, 512)
        z_ref[:, c0:c0 + 512] = (zacc * _sigmoid(zacc)).astype(BF16)
    col += Z_WIDTH
    for c0 in range(0, G_WIDTH, 512):
        gate_ref[:, c0:c0 + 512] = _sigmoid(proj(hb, col + c0, 512)).astype(BF16)


def _in_proj(x2, pos_col, g_pre, w_cat, freqs, qg, kvg, wq, wk, wv, batch, seq, *, tm=512):
    t = x2.shape[0]
    tiles_per_seq = seq // tm
    mla_width = MLA_HEADS * MLA_HEAD_LANES
    row = lambda i: (i, 0)
    const = lambda i: (0, 0)
    resident = lambda shape: pl.BlockSpec(shape, const, pipeline_mode=pl.Buffered(1))

    def qkv_spec(d):
        return pl.BlockSpec((None, d, tm // d, QKV_WIDTH),
                            lambda i: (i // tiles_per_seq, 0, i % tiles_per_seq, 0))

    return pl.pallas_call(
        functools.partial(_in_proj_kernel, tm=tm),
        grid=(t // tm,),
        in_specs=[
            pl.BlockSpec((tm, D_MODEL), row),
            pl.BlockSpec((tm, 1), row),
            resident((1, D_MODEL)),
            resident((D_MODEL, IN_TOTAL)),
            resident((1, LANES)),
            resident((1, MLA_Q_RANK)),
            resident((1, MLA_KV_RANK)),
            resident((MLA_Q_RANK, mla_width)),
            resident((MLA_KV_RANK, mla_width)),
            resident((MLA_KV_RANK, mla_width)),
        ],
        out_specs=[pl.BlockSpec((tm, mla_width), row)] * 3
        + [qkv_spec(d) for d in DIL_DILATIONS]
        + [pl.BlockSpec((tm, Z_WIDTH), row), pl.BlockSpec((tm, G_WIDTH), row)],
        out_shape=[jax.ShapeDtypeStruct((t, mla_width), BF16)] * 3
        + [jax.ShapeDtypeStruct((batch, d, seq // d, QKV_WIDTH), BF16) for d in DIL_DILATIONS]
        + [jax.ShapeDtypeStruct((t, Z_WIDTH), BF16), jax.ShapeDtypeStruct((t, G_WIDTH), BF16)],
        scratch_shapes=[
            pltpu.VMEM((N_SLABS, tm, LANES), F32),
            pltpu.VMEM((3, tm, LANES), F32),
            pltpu.VMEM((tm, D_MODEL), BF16),
        ],
        compiler_params=_params(1),
        name="in_proj",
    )(x2, pos_col, g_pre, w_cat, freqs, qg, kvg, wq, wk, wv)


_NT = (((1,), (1,)), ((), ()))


def _mla_attn_kernel(q_ref, k_ref, v_ref, o_ref, s_scr, p_scr, m_scr, alpha_scr, acc_scr, *,
                     tk, heads_per_step):
    qi = pl.program_id(2)
    tq = 2 * tk
    slices = [slice(hd * MLA_HEAD_LANES, (hd + 1) * MLA_HEAD_LANES)
              for hd in range(heads_per_step)]
    every = slice(0, tq)
    upper, lower = slice(0, tk), slice(tk, tq)

    def scores(j, slot, rows=every):
        start = pl.multiple_of(j * tk, tk)
        for hd, sl in enumerate(slices):
            s_scr[slot, hd, rows, :] = lax.dot_general(
                q_ref[rows, sl], k_ref[pl.ds(start, tk), sl], _NT, preferred_element_type=F32)

    def update(j, slot, rows=every, masked=False):
        start = pl.multiple_of(j * tk, tk)
        for hd, sl in enumerate(slices):
            for c0 in range(rows.start, rows.stop, MLA_ROW_CHUNK):
                rc = slice(c0, c0 + MLA_ROW_CHUNK)
                s = s_scr[slot, hd, rc, :]
                if masked:
                    r = lax.broadcasted_iota(jnp.int32, s.shape, 0) + (c0 - rows.start)
                    cidx = lax.broadcasted_iota(jnp.int32, s.shape, 1)
                    s = jnp.where(cidx <= r, s, NEG_BIG)
                m = m_scr[hd, rc, :]
                m_new = jnp.maximum(m, jnp.max(s, axis=-1, keepdims=True))
                p_scr[hd, rc, :] = jnp.exp2(s - m_new).astype(BF16)
                alpha_scr[hd, rc, :] = jnp.exp2(m - m_new)
                m_scr[hd, rc, :] = m_new
            acc_scr[hd, rows, :] = alpha_scr[hd, rows, :] * acc_scr[hd, rows, :] + jnp.dot(
                p_scr[hd, rows, :], v_ref[pl.ds(start, tk), sl], preferred_element_type=F32)

    m_scr[...] = jnp.full(m_scr.shape, NEG_BIG, F32)
    acc_scr[...] = jnp.zeros(acc_scr.shape, F32)
    scores(0, 0)

    def pair(jj, carry):
        j0 = 2 * jj
        scores(j0 + 1, 1)
        update(j0, 0)
        scores(j0 + 2, 0)
        update(j0 + 1, 1)
        return carry

    lax.fori_loop(0, qi, pair, 0)

    jd = 2 * qi
    scores(jd + 1, 1, lower)
    update(jd, 0, upper, masked=True)
    update(jd, 0, lower)
    update(jd + 1, 1, lower, masked=True)
    for hd in range(heads_per_step):
        acc = acc_scr[hd]
        denom = acc[:, MLA_V:MLA_V + 1]
        o_ref[:, hd * MLA_V:(hd + 1) * MLA_V] = (acc[:, :MLA_V] / denom).astype(BF16)


def _mla_attn(q, k, v, batch, seq, *, tk=512, heads_per_step=2):
    width = MLA_HEADS * MLA_HEAD_LANES
    tq = 2 * tk
    q3, k3, v3 = (a.reshape(batch, seq, width) for a in (q, k, v))
    hw = heads_per_step * MLA_HEAD_LANES
    out = pl.pallas_call(
        functools.partial(_mla_attn_kernel, tk=tk, heads_per_step=heads_per_step),
        grid=(batch, MLA_HEADS // heads_per_step, seq // tq),
        in_specs=[
            pl.BlockSpec((None, tq, hw), lambda b, hp, i: (b, i, hp)),
            pl.BlockSpec((None, seq, hw), lambda b, hp, i: (b, 0, hp)),
            pl.BlockSpec((None, seq, hw), lambda b, hp, i: (b, 0, hp)),
        ],
        out_specs=pl.BlockSpec((None, tq, heads_per_step * MLA_V), lambda b, hp, i: (b, i, hp)),
        out_shape=jax.ShapeDtypeStruct((batch, seq, MLA_WIDTH), BF16),
        scratch_shapes=[
            pltpu.VMEM((2, heads_per_step, tq, tk), F32),
            pltpu.VMEM((heads_per_step, tq, tk), BF16),
            pltpu.VMEM((heads_per_step, tq, 1), F32),
            pltpu.VMEM((heads_per_step, tq, 1), F32),
            pltpu.VMEM((heads_per_step, tq, MLA_HEAD_LANES), F32),
        ],
        compiler_params=_params(3),
        name="mla_attn",
    )(q3, k3, v3)
    return out.reshape(batch * seq, MLA_WIDTH)


def _dil_attn_kernel(q_ref, kp_ref, kc_ref, vp_ref, vc_ref, o_ref, lse_ref,
                     kk_scr, vv_scr, *, nb):
    chunk = pl.program_id(2)
    blk = DIL_BLOCK
    kk_scr[:blk, :] = kp_ref[...]
    kk_scr[blk:, :] = kc_ref[...]
    for pair in range(DIL_HPG // 2):
        src = slice(pair * LANES, (pair + 1) * LANES)
        dst = slice(2 * pair * LANES, (2 * pair + 1) * LANES)
        one = slice((2 * pair + 1) * LANES, (2 * pair + 2) * LANES)
        vv_scr[:blk, dst] = vp_ref[:, src]
        vv_scr[blk:, dst] = vc_ref[:, src]
        vv_scr[:, one] = jnp.ones((vv_scr.shape[0], LANES), BF16)

    r = lax.broadcasted_iota(jnp.int32, (blk, 2 * blk), 0)
    cidx = lax.broadcasted_iota(jnp.int32, (blk, 2 * blk), 1)
    dist = r + blk - cidx
    band = (dist >= 0) & (dist <= blk)
    lane = lax.broadcasted_iota(jnp.int32, (blk, LANES), 1)
    low = lane < DIL_DH

    for j in range(nb):
        rows = slice(j * blk, (j + 1) * blk)
        band_rows = slice(j * blk, (j + 2) * blk)
        if j == 0:
            mask = band & ((cidx >= blk) | (chunk > 0))
        else:
            mask = band
        for pair in range(DIL_HPG // 2):
            cols = slice(pair * LANES, (pair + 1) * LANES)
            q2 = q_ref[rows, cols]
            k2 = kk_scr[band_rows, cols]
            v2 = vv_scr[band_rows, 2 * pair * LANES:(2 * pair + 2) * LANES]
            outs, lses = [], []
            for keep in (low, ~low):
                qh = jnp.where(keep, q2, jnp.zeros_like(q2))
                s = lax.dot_general(qh, k2, _NT, preferred_element_type=F32)
                s = jnp.where(mask, s, NEG_BIG)
                m = jnp.max(s, axis=-1, keepdims=True)
                p = jnp.exp2(s - m)
                pv = jnp.dot(p.astype(BF16), v2, preferred_element_type=F32)
                denom = pv[:, LANES:]
                outs.append(pv[:, :LANES] / denom)
                lses.append((m + jnp.log2(denom)) * LN2)
            o_ref[rows, cols] = jnp.where(low, outs[0], outs[1]).astype(BF16)
            lse_ref[rows, cols] = jnp.where(low, lses[0], lses[1])


def _dil_attn(qkv, group):
    batch, d, sub_len, _ = qkv.shape
    nb = DIL_BLOCKS_PER_STEP[group]
    blk = DIL_BLOCK
    rows = nb * blk
    cur = lambda part: pl.BlockSpec((None, None, rows, DIL_WIDTH),
                                    lambda b, r, n: (b, r, n, part))
    prev = lambda part: pl.BlockSpec((None, None, blk, DIL_WIDTH),
                                     lambda b, r, n: (b, r, jnp.maximum(n * nb - 1, 0), part))
    out = pl.BlockSpec((None, None, rows, DIL_WIDTH), lambda b, r, n: (b, r, n, 0))
    return pl.pallas_call(
        functools.partial(_dil_attn_kernel, nb=nb),
        grid=(batch, d, sub_len // rows),
        in_specs=[cur(0), prev(1), cur(1), prev(2), cur(2)],
        out_specs=[out, out],
        out_shape=[
            jax.ShapeDtypeStruct((batch, d, sub_len, DIL_WIDTH), BF16),
            jax.ShapeDtypeStruct((batch, d, sub_len, DIL_WIDTH), F32),
        ],
        scratch_shapes=[pltpu.VMEM((rows + blk, DIL_WIDTH), BF16),
                        pltpu.VMEM((rows + blk, 2 * DIL_WIDTH), BF16)],
        compiler_params=_params(3),
        name=f"dil_attn_{group}",
    )(qkv, qkv, qkv, qkv, qkv)


def _out_kernel(x_ref, ymla_ref, o0_ref, o1_ref, o2_ref, l0_ref, l1_ref, l2_ref,
                z_ref, gate_ref, wpm_ref, wpd_ref, wo_ref, pg_ref, out_ref, perm_scr, *, tm):
    n_sl = DIL_WIDTH // LANES

    def to_token_order(ref, d, slot):
        rows = tm // d
        for r in range(d):
            for s in range(n_sl):
                perm_scr[slot * n_sl + s, pl.ds(r, rows, stride=d), :] = (
                    ref[r, :, s * LANES:(s + 1) * LANES].astype(F32))

    def token_rows(slot, rs):
        return jnp.concatenate([perm_scr[slot * n_sl + s, rs, :] for s in range(n_sl)], axis=1)

    d1, d2 = DIL_DILATIONS[1], DIL_DILATIONS[2]
    to_token_order(o1_ref, d1, 0)
    to_token_order(l1_ref, d1, 1)
    to_token_order(o2_ref, d2, 2)
    to_token_order(l2_ref, d2, 3)

    for r0 in range(0, tm, tm // 2):
        rs = slice(r0, r0 + tm // 2)
        o0, l0 = o0_ref[0, rs, :].astype(F32), l0_ref[0, rs, :]
        o1, l1 = token_rows(0, rs), token_rows(1, rs)
        o2, l2 = token_rows(2, rs), token_rows(3, rs)
        m = jnp.maximum(jnp.maximum(l0, l1), l2)
        e0, e1, e2 = jnp.exp(l0 - m), jnp.exp(l1 - m), jnp.exp(l2 - m)
        y_dil = (e0 * o0 + e1 * o1 + e2 * o2) / (e0 + e1 + e2)

        a = (ymla_ref[rs, :].astype(F32) * z_ref[rs, :MLA_WIDTH].astype(F32)).astype(BF16)
        b = (y_dil * z_ref[rs, MLA_WIDTH:].astype(F32)).astype(BF16)
        pa = jnp.dot(a, wpm_ref[...], preferred_element_type=F32)
        pb = jnp.dot(b, wpd_ref[...], preferred_element_type=F32)
        merged = (gate_ref[rs, :D_MODEL].astype(F32) * pa
                  + gate_ref[rs, D_MODEL:].astype(F32) * pb)
        u = jnp.dot(merged.astype(BF16), wo_ref[...], preferred_element_type=F32)
        out_ref[rs, :] = x_ref[rs, :] + _rms(u, pg_ref[...])


def _out_block(x2, y_mla, o_dil, lse_dil, z, gate, wpm, wpd, wo, pg, seq, *, tm=512):
    t = x2.shape[0]
    tiles_per_seq = seq // tm
    row = lambda i: (i, 0)
    const = lambda i: (0, 0)
    rows = lambda w: pl.BlockSpec((tm, w), row)
    sub = lambda d: pl.BlockSpec((None, d, tm // d, DIL_WIDTH),
                                 lambda i: (i // tiles_per_seq, 0, i % tiles_per_seq, 0))
    subs = [sub(d) for d in DIL_DILATIONS]
    return pl.pallas_call(
        functools.partial(_out_kernel, tm=tm),
        grid=(t // tm,),
        in_specs=[rows(D_MODEL), rows(MLA_WIDTH)] + subs + subs
        + [rows(Z_WIDTH), rows(G_WIDTH),
           pl.BlockSpec((MLA_WIDTH, D_MODEL), const),
           pl.BlockSpec((DIL_WIDTH, D_MODEL), const),
           pl.BlockSpec((D_MODEL, D_MODEL), const),
           pl.BlockSpec((1, D_MODEL), const)],
        out_specs=rows(D_MODEL),
        out_shape=jax.ShapeDtypeStruct((t, D_MODEL), F32),
        scratch_shapes=[pltpu.VMEM((4 * DIL_WIDTH // LANES, tm, LANES), F32)],
        compiler_params=_params(1),
        name="out_block",
    )(x2, y_mla, *o_dil, *lse_dil, z, gate, wpm, wpd, wo, pg)


def _inv_freq(rot_dim):
    return ROPE_THETA ** (-jnp.arange(0, rot_dim, 2, dtype=F32) / rot_dim)


def _lane_freqs():
    f = jnp.zeros((LANES,), F32)
    f = f.at[:DIL_ROPE].set(jnp.tile(_inv_freq(DIL_ROPE), 2))
    f = f.at[MLA_NOPE:MLA_NOPE + MLA_ROPE].set(jnp.tile(_inv_freq(MLA_ROPE), 2))
    return f[None, :]


def _layer_weights(w_in, w_uq, w_ukv):
    offs = np.cumsum([0, MLA_Q_RANK, MLA_KV_RANK, MLA_ROPE, DIL_GROUPS * QKV_WIDTH, MLA_WIDTH,
                      DIL_WIDTH, D_MODEL, D_MODEL])
    w_in = w_in.astype(BF16)
    zeros = lambda n: jnp.zeros((D_MODEL, n), BF16)
    qkv0 = int(offs[3])
    qkv = [w_in[:, qkv0 + (part * DIL_GROUPS + g) * DIL_WIDTH:
                qkv0 + (part * DIL_GROUPS + g + 1) * DIL_WIDTH]
           for g in range(DIL_GROUPS) for part in range(3)]
    w_cat = jnp.concatenate(
        [w_in[:, :offs[2]], zeros(MLA_NOPE), w_in[:, offs[2]:offs[3]],
         zeros(LANES - MLA_NOPE - MLA_ROPE)] + qkv + [w_in[:, offs[4]:]], axis=1)

    pad_heads = lambda w, n: jnp.pad(w, ((0, 0), (0, 0), (0, MLA_HEAD_LANES - n)))
    wq = pad_heads(w_uq.reshape(MLA_Q_RANK, MLA_HEADS, MLA_NOPE + MLA_ROPE), MLA_NOPE + MLA_ROPE)
    wkv = w_ukv.reshape(MLA_KV_RANK, MLA_HEADS, MLA_NOPE + MLA_V)
    wk = pad_heads(wkv[:, :, :MLA_NOPE], MLA_NOPE)
    wv = pad_heads(wkv[:, :, MLA_NOPE:], MLA_V)
    flat = lambda w: w.reshape(w.shape[0], MLA_HEADS * MLA_HEAD_LANES).astype(BF16)
    return w_cat, flat(wq), flat(wk), flat(wv)


def kernel(x, positions, pre_norm_g, w_in, q_norm_g, w_uq, kv_norm_g, w_ukv,
           w_proj_mla, w_proj_dil, w_out, post_norm_g):
    batch, seq, _ = x.shape
    depth = w_in.shape[0]
    pos_col = positions.reshape(batch * seq, 1)
    freqs = _lane_freqs()
    x2 = x.reshape(batch * seq, D_MODEL)
    for layer in range(depth):
        w_cat, wq, wk, wv = _layer_weights(w_in[layer], w_uq[layer], w_ukv[layer])
        q, k, v, qkv0, qkv1, qkv2, z, gate = _in_proj(
            x2, pos_col, pre_norm_g[layer][None, :], w_cat, freqs,
            q_norm_g[layer][None, :], kv_norm_g[layer][None, :], wq, wk, wv, batch, seq)
        y_mla = _mla_attn(q, k, v, batch, seq)
        dil = [_dil_attn(qkv, g) for g, qkv in enumerate((qkv0, qkv1, qkv2))]
        x2 = _out_block(x2, y_mla, [o for o, _ in dil], [l for _, l in dil], z, gate,
                        w_proj_mla[layer].astype(BF16), w_proj_dil[layer].astype(BF16),
                        w_out[layer].astype(BF16), post_norm_g[layer][None, :], seq)
    return x2.reshape(batch, seq, D_MODEL)
```

```python
import functools
import math

import jax
import jax.numpy as jnp
from jax import lax
from jax.experimental import pallas as pl
from jax.experimental.pallas import tpu as pltpu

F32 = jnp.float32
BF16 = jnp.bfloat16

D_MODEL = 1024
ROPE_THETA = 500000.0
NORM_EPS = 1e-6

MLA_HEADS = 8
MLA_Q_RANK = 384
MLA_KV_RANK = 256
MLA_NOPE = 64
MLA_ROPE = 32
MLA_V = 64
MLA_WIDTH = MLA_HEADS * MLA_V
MLA_HEAD_LANES = 128
MLA_ROW_CHUNK = 64

DIL_DILATIONS = (1, 4, 16)
DIL_GROUPS = 3
DIL_HPG = 8
DIL_DH = 64
DIL_ROPE = 16
DIL_WIDTH = DIL_HPG * DIL_DH
DIL_BLOCK = 128
DIL_BLOCKS_PER_STEP = (4, 4, 2)
DIL_SUBSEQ_PER_STEP = (1, 1, 2)

LANES = 128
NEG_BIG = -1e30
LOG2E = math.log2(math.e)
LN2 = math.log(2.0)

VMEM_LIMIT = 56 * 1024 * 1024

LAT_WIDTH = MLA_Q_RANK + MLA_KV_RANK + LANES
QKV_WIDTH = 3 * DIL_WIDTH
Z_WIDTH = MLA_WIDTH + DIL_WIDTH
G_WIDTH = 2 * D_MODEL
IN_TOTAL = LAT_WIDTH + DIL_GROUPS * QKV_WIDTH + Z_WIDTH + G_WIDTH
N_SLABS = D_MODEL // LANES


def _params(n_axes):
    return pltpu.CompilerParams(
        dimension_semantics=("arbitrary",) * n_axes, vmem_limit_bytes=VMEM_LIMIT)


def _trig_tables(pos_col, freq_row):
    ang = pos_col.astype(F32) * freq_row
    return jnp.cos(ang), jnp.sin(ang)


def _rope_tables(cos, sin, lane_lo, width, period):
    lane = lax.broadcasted_iota(jnp.int32, cos.shape, 1) % period
    in_a = (lane >= lane_lo) & (lane < lane_lo + width)
    in_b = (lane >= lane_lo + width) & (lane < lane_lo + 2 * width)
    c = jnp.where(in_a | in_b, cos, 1.0)
    return c, jnp.where(in_a, -sin, 0.0), jnp.where(in_b, sin, 0.0)


def _apply_rope(a, c, sa, sb, shift):
    return a * c + pltpu.roll(a, LANES - shift, 1) * sa + pltpu.roll(a, shift, 1) * sb


def _rms(x, g):
    ms = jnp.mean(x * x, axis=-1, keepdims=True)
    return x * lax.rsqrt(ms + NORM_EPS) * g


def _sigmoid(x):
    return 1.0 / (1.0 + jnp.exp(-x))


def _in_proj_kernel(x_ref, pos_ref, g_ref, w_ref, freq_ref, qg_ref, kvg_ref,
                    wq_ref, wk_ref, wv_ref,
                    q_ref, k_ref, v_ref, qkv0_ref, qkv1_ref, qkv2_ref, z_ref, gate_ref,
                    h_scr, tab_scr, hp_scr, *, tm):
    h = _rms(x_ref[...], g_ref[...])
    hb = h.astype(BF16)
    cos, sin = _trig_tables(pos_ref[...], freq_ref[...])

    def proj(lhs, c0, width):
        return jnp.dot(lhs, w_ref[:, c0:c0 + width], preferred_element_type=F32)

    lat = proj(hb, 0, LAT_WIDTH)
    cq = _rms(lat[:, :MLA_Q_RANK], qg_ref[...]).astype(BF16)
    ckv = _rms(lat[:, MLA_Q_RANK:MLA_Q_RANK + MLA_KV_RANK], kvg_ref[...]).astype(BF16)
    mhalf = MLA_ROPE // 2
    mtabs = _rope_tables(cos, sin, MLA_NOPE, mhalf, LANES)
    kr = _apply_rope(lat[:, MLA_Q_RANK + MLA_KV_RANK:], *mtabs, mhalf)
    q_scale = (MLA_NOPE + MLA_ROPE) ** -0.5 * LOG2E
    lane = lax.broadcasted_iota(jnp.int32, kr.shape, 1)
    ones_col = jnp.where(lane == MLA_V, 1.0, 0.0)
    q_all = jnp.dot(cq, wq_ref[...], preferred_element_type=F32)
    k_all = jnp.dot(ckv, wk_ref[...], preferred_element_type=F32)
    v_all = jnp.dot(ckv, wv_ref[...], preferred_element_type=F32)
    for hd in range(MLA_HEADS):
        sl = slice(hd * MLA_HEAD_LANES, (hd + 1) * MLA_HEAD_LANES)
        q_ref[:, sl] = (_apply_rope(q_all[:, sl], *mtabs, mhalf) * q_scale).astype(BF16)
        k_ref[:, sl] = (k_all[:, sl] + kr).astype(BF16)
        v_ref[:, sl] = (v_all[:, sl] + ones_col).astype(BF16)

    dhalf = DIL_ROPE // 2
    first = lax.broadcasted_iota(jnp.int32, cos.shape, 1) < DIL_DH
    dtabs = _rope_tables(jnp.where(first, cos, pltpu.roll(cos, DIL_DH, 1)),
                         jnp.where(first, sin, pltpu.roll(sin, DIL_DH, 1)), 0, dhalf, DIL_DH)
    d_scale = DIL_DH ** -0.5 * LOG2E

    def qkv_pieces(lhs, group, tables):
        for part in range(3):
            acc = proj(lhs, LAT_WIDTH + (part * DIL_GROUPS + group) * DIL_WIDTH, DIL_WIDTH)
            for l0 in range(0, DIL_WIDTH, LANES):
                a = acc[:, l0:l0 + LANES]
                if part < 2:
                    a = _apply_rope(a, *tables, dhalf)
                if part == 0:
                    a = a * d_scale
                yield part * DIL_WIDTH + l0, a.astype(BF16)

    for off, piece in qkv_pieces(hb, 0, dtabs):
        qkv0_ref[0, :, off:off + LANES] = piece

    for s in range(N_SLABS):
        h_scr[s] = h[:, s * LANES:(s + 1) * LANES]
    for i, t in enumerate(dtabs):
        tab_scr[i] = t
    for group, out_ref in ((1, qkv1_ref), (2, qkv2_ref)):
        d = DIL_DILATIONS[group]
        rows = tm // d
        for r in range(d):
            for s in range(N_SLABS):
                hp_scr[r * rows:(r + 1) * rows, s * LANES:(s + 1) * LANES] = (
                    h_scr[s, pl.ds(r, rows, stride=d), :].astype(BF16))
        ptabs = tuple(
            jnp.concatenate([tab_scr[i, pl.ds(r, rows, stride=d), :] for r in range(d)], axis=0)
            for i in range(3))
        for off, piece in qkv_pieces(hp_scr[...], group, ptabs):
            for r in range(d):
                out_ref[r, :, off:off + LANES] = piece[r * rows:(r + 1) * rows, :]

    col = LAT_WIDTH + DIL_GROUPS * QKV_WIDTH
    for c0 in range(0, Z_WIDTH, 512):
        zacc = proj(hb, col + c0, 512)
        z_ref[:, c0:c0 + 512] = (zacc * _sigmoid(zacc)).astype(BF16)
    col += Z_WIDTH
    for c0 in range(0, G_WIDTH, 512):
        gate_ref[:, c0:c0 + 512] = _sigmoid(proj(hb, col + c0, 512)).astype(BF16)


def _in_proj(x2, pos_col, g_pre, w_cat, freqs, qg, kvg, wq, wk, wv, batch, seq, *, tm=512):
    t = x2.shape[0]
    tiles_per_seq = seq // tm
    mla_width = MLA_HEADS * MLA_HEAD_LANES
    row = lambda i: (i, 0)
    const = lambda i: (0, 0)
    resident = lambda shape: pl.BlockSpec(shape, const, pipeline_mode=pl.Buffered(1))

    def qkv_spec(d):
        return pl.BlockSpec((None, d, tm // d, QKV_WIDTH),
                            lambda i: (i // tiles_per_seq, 0, i % tiles_per_seq, 0))

    return pl.pallas_call(
        functools.partial(_in_proj_kernel, tm=tm),
        grid=(t // tm,),
        in_specs=[
            pl.BlockSpec((tm, D_MODEL), row),
            pl.BlockSpec((tm, 1), row),
            resident((1, D_MODEL)),
            resident((D_MODEL, IN_TOTAL)),
            resident((1, LANES)),
            resident((1, MLA_Q_RANK)),
            resident((1, MLA_KV_RANK)),
            resident((MLA_Q_RANK, mla_width)),
            resident((MLA_KV_RANK, mla_width)),
            resident((MLA_KV_RANK, mla_width)),
        ],
        out_specs=[pl.BlockSpec((tm, mla_width), row)] * 3
        + [qkv_spec(d) for d in DIL_DILATIONS]
        + [pl.BlockSpec((tm, Z_WIDTH), row), pl.BlockSpec((tm, G_WIDTH), row)],
        out_shape=[jax.ShapeDtypeStruct((t, mla_width), BF16)] * 3
        + [jax.ShapeDtypeStruct((batch, d, seq // d, QKV_WIDTH), BF16) for d in DIL_DILATIONS]
        + [jax.ShapeDtypeStruct((t, Z_WIDTH), BF16), jax.ShapeDtypeStruct((t, G_WIDTH), BF16)],
        scratch_shapes=[
            pltpu.VMEM((N_SLABS, tm, LANES), F32),
            pltpu.VMEM((3, tm, LANES), F32),
            pltpu.VMEM((tm, D_MODEL), BF16),
        ],
        compiler_params=_params(1),
        name="in_proj",
    )(x2, pos_col, g_pre, w_cat, freqs, qg, kvg, wq, wk, wv)


_NT = (((1,), (1,)), ((), ()))


def _mla_attn_kernel(q_ref, k_ref, v_ref, o_ref, s_scr, p_scr, m_scr, alpha_scr, acc_scr, *,
                     tk, heads_per_step):
    qi = pl.program_id(2)
    tq = 2 * tk
    slices = [slice(hd * MLA_HEAD_LANES, (hd + 1) * MLA_HEAD_LANES)
              for hd in range(heads_per_step)]
    every = slice(0, tq)
    upper, lower = slice(0, tk), slice(tk, tq)

    def scores(j, slot, rows=every):
        start = pl.multiple_of(j * tk, tk)
        for hd, sl in enumerate(slices):
            s_scr[slot, hd, rows, :] = lax.dot_general(
                q_ref[rows, sl], k_ref[pl.ds(start, tk), sl], _NT, preferred_element_type=F32)

    def update(j, slot, rows=every, masked=False):
        start = pl.multiple_of(j * tk, tk)
        for hd, sl in enumerate(slices):
            for c0 in range(rows.start, rows.stop, MLA_ROW_CHUNK):
                rc = slice(c0, c0 + MLA_ROW_CHUNK)
                s = s_scr[slot, hd, rc, :]
                if masked:
                    r = lax.broadcasted_iota(jnp.int32, s.shape, 0) + (c0 - rows.start)
                    cidx = lax.broadcasted_iota(jnp.int32, s.shape, 1)
                    s = jnp.where(cidx <= r, s, NEG_BIG)
                m = m_scr[hd, rc, :]
                m_new = jnp.maximum(m, jnp.max(s, axis=-1, keepdims=True))
                p_scr[hd, rc, :] = jnp.exp2(s - m_new).astype(BF16)
                alpha_scr[hd, rc, :] = jnp.exp2(m - m_new)
                m_scr[hd, rc, :] = m_new
            acc_scr[hd, rows, :] = alpha_scr[hd, rows, :] * acc_scr[hd, rows, :] + jnp.dot(
                p_scr[hd, rows, :], v_ref[pl.ds(start, tk), sl], preferred_element_type=F32)

    m_scr[...] = jnp.full(m_scr.shape, NEG_BIG, F32)
    acc_scr[...] = jnp.zeros(acc_scr.shape, F32)
    scores(0, 0)

    def pair(jj, carry):
        j0 = 2 * jj
        scores(j0 + 1, 1)
        update(j0, 0)
        scores(j0 + 2, 0)
        update(j0 + 1, 1)
        return carry

    lax.fori_loop(0, qi, pair, 0)

    jd = 2 * qi
    scores(jd + 1, 1, lower)
    update(jd, 0, upper, masked=True)
    update(jd, 0, lower)
    update(jd + 1, 1, lower, masked=True)
    for hd in range(heads_per_step):
        acc = acc_scr[hd]
        denom = acc[:, MLA_V:MLA_V + 1]
        o_ref[:, hd * MLA_V:(hd + 1) * MLA_V] = (acc[:, :MLA_V] / denom).astype(BF16)


def _mla_attn(q, k, v, batch, seq, *, tk=512, heads_per_step=2):
    width = MLA_HEADS * MLA_HEAD_LANES
    tq = 2 * tk
    q3, k3, v3 = (a.reshape(batch, seq, width) for a in (q, k, v))
    hw = heads_per_step * MLA_HEAD_LANES
    out = pl.pallas_call(
        functools.partial(_mla_attn_kernel, tk=tk, heads_per_step=heads_per_step),
        grid=(batch, MLA_HEADS // heads_per_step, seq // tq),
        in_specs=[
            pl.BlockSpec((None, tq, hw), lambda b, hp, i: (b, i, hp)),
            pl.BlockSpec((None, seq, hw), lambda b, hp, i: (b, 0, hp)),
            pl.BlockSpec((None, seq, hw), lambda b, hp, i: (b, 0, hp)),
        ],
        out_specs=pl.BlockSpec((None, tq, heads_per_step * MLA_V), lambda b, hp, i: (b, i, hp)),
        out_shape=jax.ShapeDtypeStruct((batch, seq, MLA_WIDTH), BF16),
        scratch_shapes=[
            pltpu.VMEM((2, heads_per_step, tq, tk), F32),
            pltpu.VMEM((heads_per_step, tq, tk), BF16),
            pltpu.VMEM((heads_per_step, tq, 1), F32),
            pltpu.VMEM((heads_per_step, tq, 1), F32),
            pltpu.VMEM((heads_per_step, tq, MLA_HEAD_LANES), F32),
        ],
        compiler_params=_params(3),
        name="mla_attn",
    )(q3, k3, v3)
    return out.reshape(batch * seq, MLA_WIDTH)


def _dil_attn_kernel(q_ref, kp_ref, kc_ref, vp_ref, vc_ref, o_ref, lse_ref,
                     kk_scr, vv_scr, *, nb, nr):
    chunk = pl.program_id(2)
    blk = DIL_BLOCK
    kk_scr[:, :blk, :] = kp_ref[...]
    kk_scr[:, blk:, :] = kc_ref[...]
    for pair in range(DIL_HPG // 2):
        src = slice(pair * LANES, (pair + 1) * LANES)
        dst = slice(2 * pair * LANES, (2 * pair + 1) * LANES)
        one = slice((2 * pair + 1) * LANES, (2 * pair + 2) * LANES)
        vv_scr[:, :blk, dst] = vp_ref[:, :, src]
        vv_scr[:, blk:, dst] = vc_ref[:, :, src]
        vv_scr[:, :, one] = jnp.ones(vv_scr.shape[:2] + (LANES,), BF16)

    r = lax.broadcasted_iota(jnp.int32, (blk, 2 * blk), 0)
    cidx = lax.broadcasted_iota(jnp.int32, (blk, 2 * blk), 1)
    dist = r + blk - cidx
    band = (dist >= 0) & (dist <= blk)
    lane = lax.broadcasted_iota(jnp.int32, (blk, LANES), 1)
    low = lane < DIL_DH

    for ri in range(nr):
        for j in range(nb):
            rows = slice(j * blk, (j + 1) * blk)
            band_rows = slice(j * blk, (j + 2) * blk)
            if j == 0:
                mask = band & ((cidx >= blk) | (chunk > 0))
            else:
                mask = band
            for pair in range(DIL_HPG // 2):
                cols = slice(pair * LANES, (pair + 1) * LANES)
                q2 = q_ref[ri, rows, cols]
                k2 = kk_scr[ri, band_rows, cols]
                v2 = vv_scr[ri, band_rows, 2 * pair * LANES:(2 * pair + 2) * LANES]
                pvs, ms = [], []
                for keep in (low, ~low):
                    qh = jnp.where(keep, q2, jnp.zeros_like(q2))
                    s = lax.dot_general(qh, k2, _NT, preferred_element_type=F32)
                    s = jnp.where(mask, s, NEG_BIG)
                    m = jnp.max(s, axis=-1, keepdims=True)
                    p = jnp.exp2(s - m)
                    pvs.append(jnp.dot(p.astype(BF16), v2, preferred_element_type=F32))
                    ms.append(m)
                num = jnp.where(low, pvs[0][:, :LANES], pvs[1][:, :LANES])
                denom = jnp.where(low, pvs[0][:, LANES:], pvs[1][:, LANES:])
                o_ref[ri, rows, cols] = (num / denom).astype(BF16)
                lse_ref[ri, rows, cols] = (jnp.where(low, ms[0], ms[1]) + jnp.log2(denom)) * LN2


def _dil_attn(qkv, group):
    batch, d, sub_len, _ = qkv.shape
    nb = DIL_BLOCKS_PER_STEP[group]
    nr = DIL_SUBSEQ_PER_STEP[group]
    blk = DIL_BLOCK
    rows = nb * blk
    cur = lambda part: pl.BlockSpec((None, nr, rows, DIL_WIDTH),
                                    lambda b, r, n: (b, r, n, part))
    prev = lambda part: pl.BlockSpec((None, nr, blk, DIL_WIDTH),
                                     lambda b, r, n: (b, r, jnp.maximum(n * nb - 1, 0), part))
    out = pl.BlockSpec((None, nr, rows, DIL_WIDTH), lambda b, r, n: (b, r, n, 0))
    return pl.pallas_call(
        functools.partial(_dil_attn_kernel, nb=nb, nr=nr),
        grid=(batch, d // nr, sub_len // rows),
        in_specs=[cur(0), prev(1), cur(1), prev(2), cur(2)],
        out_specs=[out, out],
        out_shape=[
            jax.ShapeDtypeStruct((batch, d, sub_len, DIL_WIDTH), BF16),
            jax.ShapeDtypeStruct((batch, d, sub_len, DIL_WIDTH), F32),
        ],
        scratch_shapes=[pltpu.VMEM((nr, rows + blk, DIL_WIDTH), BF16),
                        pltpu.VMEM((nr, rows + blk, 2 * DIL_WIDTH), BF16)],
        compiler_params=_params(3),
        name=f"dil_attn_{group}",
    )(qkv, qkv, qkv, qkv, qkv)


def _out_kernel(x_ref, ymla_ref, o0_ref, o1_ref, o2_ref, l0_ref, l1_ref, l2_ref,
                z_ref, gate_ref, wpm_ref, wpd_ref, wo_ref, pg_ref, out_ref, perm_scr, *, tm):
    n_sl = DIL_WIDTH // LANES

    def to_token_order(ref, d, slot):
        rows = tm // d
        for r in range(d):
            for s in range(n_sl):
                perm_scr[slot * n_sl + s, pl.ds(r, rows, stride=d), :] = (
                    ref[r, :, s * LANES:(s + 1) * LANES].astype(F32))

    def token_rows(slot, rs):
        return jnp.concatenate([perm_scr[slot * n_sl + s, rs, :] for s in range(n_sl)], axis=1)

    d1, d2 = DIL_DILATIONS[1], DIL_DILATIONS[2]
    to_token_order(o1_ref, d1, 0)
    to_token_order(l1_ref, d1, 1)
    to_token_order(o2_ref, d2, 2)
    to_token_order(l2_ref, d2, 3)

    for r0 in range(0, tm, tm // 2):
        rs = slice(r0, r0 + tm // 2)
        o0, l0 = o0_ref[0, rs, :].astype(F32), l0_ref[0, rs, :]
        o1, l1 = token_rows(0, rs), token_rows(1, rs)
        o2, l2 = token_rows(2, rs), token_rows(3, rs)
        m = jnp.maximum(jnp.maximum(l0, l1), l2)
        e0, e1, e2 = jnp.exp(l0 - m), jnp.exp(l1 - m), jnp.exp(l2 - m)
        y_dil = (e0 * o0 + e1 * o1 + e2 * o2) / (e0 + e1 + e2)

        a = (ymla_ref[rs, :].astype(F32) * z_ref[rs, :MLA_WIDTH].astype(F32)).astype(BF16)
        b = (y_dil * z_ref[rs, MLA_WIDTH:].astype(F32)).astype(BF16)
        pa = jnp.dot(a, wpm_ref[...], preferred_element_type=F32)
        pb = jnp.dot(b, wpd_ref[...], preferred_element_type=F32)
        merged = (gate_ref[rs, :D_MODEL].astype(F32) * pa
                  + gate_ref[rs, D_MODEL:].astype(F32) * pb)
        u = jnp.dot(merged.astype(BF16), wo_ref[...], preferred_element_type=F32)
        out_ref[rs, :] = x_ref[rs, :] + _rms(u, pg_ref[...])


def _out_block(x2, y_mla, o_dil, lse_dil, z, gate, wpm, wpd, wo, pg, seq, *, tm=512):
    t = x2.shape[0]
    tiles_per_seq = seq // tm
    row = lambda i: (i, 0)
    const = lambda i: (0, 0)
    rows = lambda w: pl.BlockSpec((tm, w), row)
    sub = lambda d: pl.BlockSpec((None, d, tm // d, DIL_WIDTH),
                                 lambda i: (i // tiles_per_seq, 0, i % tiles_per_seq, 0))
    subs = [sub(d) for d in DIL_DILATIONS]
    return pl.pallas_call(
        functools.partial(_out_kernel, tm=tm),
        grid=(t // tm,),
        in_specs=[rows(D_MODEL), rows(MLA_WIDTH)] + subs + subs
        + [rows(Z_WIDTH), rows(G_WIDTH),
           pl.BlockSpec((MLA_WIDTH, D_MODEL), const),
           pl.BlockSpec((DIL_WIDTH, D_MODEL), const),
           pl.BlockSpec((D_MODEL, D_MODEL), const),
           pl.BlockSpec((1, D_MODEL), const)],
        out_specs=rows(D_MODEL),
        out_shape=jax.ShapeDtypeStruct((t, D_MODEL), F32),
        scratch_shapes=[pltpu.VMEM((4 * DIL_WIDTH // LANES, tm, LANES), F32)],
        compiler_params=_params(1),
        name="out_block",
    )(x2, y_mla, *o_dil, *lse_dil, z, gate, wpm, wpd, wo, pg)


def _inv_freq(rot_dim):
    return ROPE_THETA ** (-jnp.arange(0, rot_dim, 2, dtype=F32) / rot_dim)


def _lane_freqs():
    f = jnp.zeros((LANES,), F32)
    f = f.at[:DIL_ROPE].set(jnp.tile(_inv_freq(DIL_ROPE), 2))
    f = f.at[MLA_NOPE:MLA_NOPE + MLA_ROPE].set(jnp.tile(_inv_freq(MLA_ROPE), 2))
    return f[None, :]


def _w_cat_kernel(w_ref, o_ref):
    lat = MLA_Q_RANK + MLA_KV_RANK
    rows = w_ref.shape[0]
    o_ref[:, :lat] = w_ref[:, :lat].astype(BF16)
    o_ref[:, lat:lat + MLA_NOPE] = jnp.zeros((rows, MLA_NOPE), BF16)
    o_ref[:, lat + MLA_NOPE:lat + MLA_NOPE + MLA_ROPE] = w_ref[:, lat:lat + MLA_ROPE].astype(BF16)
    o_ref[:, lat + MLA_NOPE + MLA_ROPE:LAT_WIDTH] = jnp.zeros(
        (rows, LANES - MLA_NOPE - MLA_ROPE), BF16)
    o_ref[:, LAT_WIDTH:] = w_ref[:, lat + MLA_ROPE:].astype(BF16)


def _w_cat(w_in, *, rows=128):
    k, n = w_in.shape
    return pl.pallas_call(
        _w_cat_kernel,
        grid=(k // rows,),
        in_specs=[pl.BlockSpec((rows, n), lambda i: (i, 0))],
        out_specs=pl.BlockSpec((rows, IN_TOTAL), lambda i: (i, 0)),
        out_shape=jax.ShapeDtypeStruct((k, IN_TOTAL), BF16),
        compiler_params=_params(1),
        name="w_cat",
    )(w_in)


def _layer_weights(w_in, w_uq, w_ukv):
    w_cat = _w_cat(w_in)

    pad_heads = lambda w, n: jnp.pad(w, ((0, 0), (0, 0), (0, MLA_HEAD_LANES - n)))
    wq = pad_heads(w_uq.reshape(MLA_Q_RANK, MLA_HEADS, MLA_NOPE + MLA_ROPE), MLA_NOPE + MLA_ROPE)
    wkv = w_ukv.reshape(MLA_KV_RANK, MLA_HEADS, MLA_NOPE + MLA_V)
    wk = pad_heads(wkv[:, :, :MLA_NOPE], MLA_NOPE)
    wv = pad_heads(wkv[:, :, MLA_NOPE:], MLA_V)
    flat = lambda w: w.reshape(w.shape[0], MLA_HEADS * MLA_HEAD_LANES).astype(BF16)
    return w_cat, flat(wq), flat(wk), flat(wv)


def kernel(x, positions, pre_norm_g, w_in, q_norm_g, w_uq, kv_norm_g, w_ukv,
           w_proj_mla, w_proj_dil, w_out, post_norm_g):
    batch, seq, _ = x.shape
    depth = w_in.shape[0]
    pos_col = positions.reshape(batch * seq, 1)
    freqs = _lane_freqs()
    x2 = x.reshape(batch * seq, D_MODEL)
    for layer in range(depth):
        w_cat, wq, wk, wv = _layer_weights(w_in[layer], w_uq[layer], w_ukv[layer])
        q, k, v, qkv0, qkv1, qkv2, z, gate = _in_proj(
            x2, pos_col, pre_norm_g[layer][None, :], w_cat, freqs,
            q_norm_g[layer][None, :], kv_norm_g[layer][None, :], wq, wk, wv, batch, seq)
        y_mla = _mla_attn(q, k, v, batch, seq)
        dil = [_dil_attn(qkv, g) for g, qkv in enumerate((qkv0, qkv1, qkv2))]
        x2 = _out_block(x2, y_mla, [o for o, _ in dil], [l for _, l in dil], z, gate,
                        w_proj_mla[layer].astype(BF16), w_proj_dil[layer].astype(BF16),
                        w_out[layer].astype(BF16), post_norm_g[layer][None, :], seq)
    return x2.reshape(batch, seq, D_MODEL)
```

```python
import functools
import math

import jax
import jax.numpy as jnp
from jax import lax
from jax.experimental import pallas as pl
from jax.experimental.pallas import tpu as pltpu

F32 = jnp.float32
BF16 = jnp.bfloat16

D_MODEL = 1024
ROPE_THETA = 500000.0
NORM_EPS = 1e-6

MLA_HEADS = 8
MLA_Q_RANK = 384
MLA_KV_RANK = 256
MLA_NOPE = 64
MLA_ROPE = 32
MLA_V = 64
MLA_WIDTH = MLA_HEADS * MLA_V
MLA_HEAD_LANES = 128
MLA_ROW_CHUNK = 64

DIL_DILATIONS = (1, 4, 16)
DIL_GROUPS = 3
DIL_HPG = 8
DIL_DH = 64
DIL_ROPE = 16
DIL_WIDTH = DIL_HPG * DIL_DH
DIL_BLOCK = 128
DIL_BLOCKS_PER_STEP = (4, 4, 2)
DIL_SUBSEQ_PER_STEP = (1, 1, 2)

LANES = 128
NEG_BIG = -1e30
LOG2E = math.log2(math.e)
LN2 = math.log(2.0)

VMEM_LIMIT = 56 * 1024 * 1024

LAT_WIDTH = MLA_Q_RANK + MLA_KV_RANK + LANES
QKV_WIDTH = 3 * DIL_WIDTH
Z_WIDTH = MLA_WIDTH + DIL_WIDTH
G_WIDTH = 2 * D_MODEL
IN_TOTAL = LAT_WIDTH + DIL_GROUPS * QKV_WIDTH + Z_WIDTH + G_WIDTH
N_SLABS = D_MODEL // LANES


def _params(n_axes):
    return pltpu.CompilerParams(
        dimension_semantics=("arbitrary",) * n_axes, vmem_limit_bytes=VMEM_LIMIT)


def _trig_tables(pos_col, freq_row):
    ang = pos_col.astype(F32) * freq_row
    return jnp.cos(ang), jnp.sin(ang)


def _rope_tables(cos, sin, lane_lo, width, period):
    lane = lax.broadcasted_iota(jnp.int32, cos.shape, 1) % period
    in_a = (lane >= lane_lo) & (lane < lane_lo + width)
    in_b = (lane >= lane_lo + width) & (lane < lane_lo + 2 * width)
    c = jnp.where(in_a | in_b, cos, 1.0)
    return c, jnp.where(in_a, -sin, 0.0), jnp.where(in_b, sin, 0.0)


def _apply_rope(a, c, sa, sb, shift):
    return a * c + pltpu.roll(a, LANES - shift, 1) * sa + pltpu.roll(a, shift, 1) * sb


def _rms(x, g):
    ms = jnp.mean(x * x, axis=-1, keepdims=True)
    return x * lax.rsqrt(ms + NORM_EPS) * g


def _sigmoid(x):
    return 1.0 / (1.0 + jnp.exp(-x))


def _in_proj_kernel(x_ref, pos_ref, g_ref, w_ref, freq_ref, qg_ref, kvg_ref,
                    wq_ref, wkv_ref,
                    q_ref, k_ref, v_ref, qkv0_ref, qkv1_ref, qkv2_ref, z_ref, gate_ref,
                    h_scr, tab_scr, hp_scr, *, tm):
    h = _rms(x_ref[...], g_ref[...])
    hb = h.astype(BF16)
    cos, sin = _trig_tables(pos_ref[...], freq_ref[...])

    def proj(lhs, c0, width):
        return lax.dot_general(lhs, w_ref[c0:c0 + width, :], _NT, preferred_element_type=F32)

    lat = proj(hb, 0, LAT_WIDTH)
    cq = _rms(lat[:, :MLA_Q_RANK], qg_ref[...]).astype(BF16)
    ckv = _rms(lat[:, MLA_Q_RANK:MLA_Q_RANK + MLA_KV_RANK], kvg_ref[...]).astype(BF16)
    mhalf = MLA_ROPE // 2
    mtabs = _rope_tables(cos, sin, MLA_NOPE, mhalf, LANES)
    kr = _apply_rope(lat[:, MLA_Q_RANK + MLA_KV_RANK:], *mtabs, mhalf)
    q_scale = (MLA_NOPE + MLA_ROPE) ** -0.5 * LOG2E
    lane = lax.broadcasted_iota(jnp.int32, kr.shape, 1)
    q_all = jnp.dot(cq, wq_ref[...], preferred_element_type=F32)
    kv_all = jnp.dot(ckv, wkv_ref[...], preferred_element_type=F32)
    for hd in range(MLA_HEADS):
        sl = slice(hd * MLA_HEAD_LANES, (hd + 1) * MLA_HEAD_LANES)
        q_ref[:, sl] = (_apply_rope(q_all[:, sl], *mtabs, mhalf) * q_scale).astype(BF16)
        kv = kv_all[:, sl]
        k_ref[:, sl] = jnp.where(lane < MLA_NOPE, kv, kr).astype(BF16)
        v_ref[:, sl] = jnp.where(lane == MLA_V, 1.0, pltpu.roll(kv, MLA_NOPE, 1)).astype(BF16)

    dhalf = DIL_ROPE // 2
    first = lax.broadcasted_iota(jnp.int32, cos.shape, 1) < DIL_DH
    dtabs = _rope_tables(jnp.where(first, cos, pltpu.roll(cos, DIL_DH, 1)),
                         jnp.where(first, sin, pltpu.roll(sin, DIL_DH, 1)), 0, dhalf, DIL_DH)
    d_scale = DIL_DH ** -0.5 * LOG2E

    def qkv_pieces(lhs, group, tables):
        for part in range(3):
            acc = proj(lhs, LAT_WIDTH + (part * DIL_GROUPS + group) * DIL_WIDTH, DIL_WIDTH)
            for l0 in range(0, DIL_WIDTH, LANES):
                a = acc[:, l0:l0 + LANES]
                if part < 2:
                    a = _apply_rope(a, *tables, dhalf)
                if part == 0:
                    a = a * d_scale
                yield part * DIL_WIDTH + l0, a.astype(BF16)

    for off, piece in qkv_pieces(hb, 0, dtabs):
        qkv0_ref[0, :, off:off + LANES] = piece

    for s in range(N_SLABS):
        h_scr[s] = h[:, s * LANES:(s + 1) * LANES]
    for i, t in enumerate(dtabs):
        tab_scr[i] = t
    for group, out_ref in ((1, qkv1_ref), (2, qkv2_ref)):
        d = DIL_DILATIONS[group]
        rows = tm // d
        for r in range(d):
            for s in range(N_SLABS):
                hp_scr[r * rows:(r + 1) * rows, s * LANES:(s + 1) * LANES] = (
                    h_scr[s, pl.ds(r, rows, stride=d), :].astype(BF16))
        ptabs = tuple(
            jnp.concatenate([tab_scr[i, pl.ds(r, rows, stride=d), :] for r in range(d)], axis=0)
            for i in range(3))
        for off, piece in qkv_pieces(hp_scr[...], group, ptabs):
            for r in range(d):
                out_ref[r, :, off:off + LANES] = piece[r * rows:(r + 1) * rows, :]

    col = LAT_WIDTH + DIL_GROUPS * QKV_WIDTH
    for c0 in range(0, Z_WIDTH, 512):
        zacc = proj(hb, col + c0, 512)
        z_ref[:, c0:c0 + 512] = (zacc * _sigmoid(zacc)).astype(BF16)
    col += Z_WIDTH
    for c0 in range(0, G_WIDTH, 512):
        gate_ref[:, c0:c0 + 512] = _sigmoid(proj(hb, col + c0, 512)).astype(BF16)


def _in_proj(x2, pos_col, g_pre, w_cat, freqs, qg, kvg, wq, wkv, batch, seq, *, tm=512):
    t = x2.shape[0]
    tiles_per_seq = seq // tm
    mla_width = MLA_HEADS * MLA_HEAD_LANES
    row = lambda i: (i, 0)
    const = lambda i: (0, 0)
    resident = lambda shape: pl.BlockSpec(shape, const, pipeline_mode=pl.Buffered(1))

    def qkv_spec(d):
        return pl.BlockSpec((None, d, tm // d, QKV_WIDTH),
                            lambda i: (i // tiles_per_seq, 0, i % tiles_per_seq, 0))

    return pl.pallas_call(
        functools.partial(_in_proj_kernel, tm=tm),
        grid=(t // tm,),
        in_specs=[
            pl.BlockSpec((tm, D_MODEL), row),
            pl.BlockSpec((tm, 1), row),
            resident((1, D_MODEL)),
            resident((IN_TOTAL, D_MODEL)),
            resident((1, LANES)),
            resident((1, MLA_Q_RANK)),
            resident((1, MLA_KV_RANK)),
            resident((MLA_Q_RANK, mla_width)),
            resident((MLA_KV_RANK, mla_width)),
        ],
        out_specs=[pl.BlockSpec((tm, mla_width), row)] * 3
        + [qkv_spec(d) for d in DIL_DILATIONS]
        + [pl.BlockSpec((tm, Z_WIDTH), row), pl.BlockSpec((tm, G_WIDTH), row)],
        out_shape=[jax.ShapeDtypeStruct((t, mla_width), BF16)] * 3
        + [jax.ShapeDtypeStruct((batch, d, seq // d, QKV_WIDTH), BF16) for d in DIL_DILATIONS]
        + [jax.ShapeDtypeStruct((t, Z_WIDTH), BF16), jax.ShapeDtypeStruct((t, G_WIDTH), BF16)],
        scratch_shapes=[
            pltpu.VMEM((N_SLABS, tm, LANES), F32),
            pltpu.VMEM((3, tm, LANES), F32),
            pltpu.VMEM((tm, D_MODEL), BF16),
        ],
        compiler_params=_params(1),
        name="in_proj",
    )(x2, pos_col, g_pre, w_cat, freqs, qg, kvg, wq, wkv)


_NT = (((1,), (1,)), ((), ()))


def _mla_attn_kernel(q_ref, k_ref, v_ref, o_ref, s_scr, p_scr, m_scr, alpha_scr, acc_scr, *,
                     tk, heads_per_step):
    qi = pl.program_id(2)
    tq = 2 * tk
    slices = [slice(hd * MLA_HEAD_LANES, (hd + 1) * MLA_HEAD_LANES)
              for hd in range(heads_per_step)]
    every = slice(0, tq)
    upper, lower = slice(0, tk), slice(tk, tq)

    def scores(j, slot, rows=every):
        start = pl.multiple_of(j * tk, tk)
        for hd, sl in enumerate(slices):
            s_scr[slot, hd, rows, :] = lax.dot_general(
                q_ref[rows, sl], k_ref[pl.ds(start, tk), sl], _NT, preferred_element_type=F32)

    def update(j, slot, rows=every, masked=False):
        start = pl.multiple_of(j * tk, tk)
        for hd, sl in enumerate(slices):
            for c0 in range(rows.start, rows.stop, MLA_ROW_CHUNK):
                rc = slice(c0, c0 + MLA_ROW_CHUNK)
                s = s_scr[slot, hd, rc, :]
                if masked:
                    r = lax.broadcasted_iota(jnp.int32, s.shape, 0) + (c0 - rows.start)
                    cidx = lax.broadcasted_iota(jnp.int32, s.shape, 1)
                    s = jnp.where(cidx <= r, s, NEG_BIG)
                m = m_scr[hd, rc, :]
                m_new = jnp.maximum(m, jnp.max(s, axis=-1, keepdims=True))
                p_scr[hd, rc, :] = jnp.exp2(s - m_new).astype(BF16)
                alpha_scr[hd, rc, :] = jnp.exp2(m - m_new)
                m_scr[hd, rc, :] = m_new
            acc_scr[hd, rows, :] = alpha_scr[hd, rows, :] * acc_scr[hd, rows, :] + jnp.dot(
                p_scr[hd, rows, :], v_ref[pl.ds(start, tk), sl], preferred_element_type=F32)

    m_scr[...] = jnp.full(m_scr.shape, NEG_BIG, F32)
    acc_scr[...] = jnp.zeros(acc_scr.shape, F32)
    scores(0, 0)

    def pair(jj, carry):
        j0 = 2 * jj
        scores(j0 + 1, 1)
        update(j0, 0)
        scores(j0 + 2, 0)
        update(j0 + 1, 1)
        return carry

    lax.fori_loop(0, qi, pair, 0)

    jd = 2 * qi
    scores(jd + 1, 1, lower)
    update(jd, 0, upper, masked=True)
    update(jd, 0, lower)
    update(jd + 1, 1, lower, masked=True)
    for hd in range(heads_per_step):
        acc = acc_scr[hd]
        denom = acc[:, MLA_V:MLA_V + 1]
        o_ref[:, hd * MLA_V:(hd + 1) * MLA_V] = (acc[:, :MLA_V] / denom).astype(BF16)


def _mla_attn(q, k, v, batch, seq, *, tk=512, heads_per_step=2):
    width = MLA_HEADS * MLA_HEAD_LANES
    tq = 2 * tk
    q3, k3, v3 = (a.reshape(batch, seq, width) for a in (q, k, v))
    hw = heads_per_step * MLA_HEAD_LANES
    out = pl.pallas_call(
        functools.partial(_mla_attn_kernel, tk=tk, heads_per_step=heads_per_step),
        grid=(batch, MLA_HEADS // heads_per_step, seq // tq),
        in_specs=[
            pl.BlockSpec((None, tq, hw), lambda b, hp, i: (b, i, hp)),
            pl.BlockSpec((None, seq, hw), lambda b, hp, i: (b, 0, hp)),
            pl.BlockSpec((None, seq, hw), lambda b, hp, i: (b, 0, hp)),
        ],
        out_specs=pl.BlockSpec((None, tq, heads_per_step * MLA_V), lambda b, hp, i: (b, i, hp)),
        out_shape=jax.ShapeDtypeStruct((batch, seq, MLA_WIDTH), BF16),
        scratch_shapes=[
            pltpu.VMEM((2, heads_per_step, tq, tk), F32),
            pltpu.VMEM((heads_per_step, tq, tk), BF16),
            pltpu.VMEM((heads_per_step, tq, 1), F32),
            pltpu.VMEM((heads_per_step, tq, 1), F32),
            pltpu.VMEM((heads_per_step, tq, MLA_HEAD_LANES), F32),
        ],
        compiler_params=_params(3),
        name="mla_attn",
    )(q3, k3, v3)
    return out.reshape(batch * seq, MLA_WIDTH)


def _dil_attn_kernel(q_ref, kp_ref, kc_ref, vp_ref, vc_ref, o_ref, lse_ref,
                     kk_scr, vv_scr, *, nb, nr):
    chunk = pl.program_id(2)
    blk = DIL_BLOCK
    kk_scr[:, :blk, :] = kp_ref[...]
    kk_scr[:, blk:, :] = kc_ref[...]
    for pair in range(DIL_HPG // 2):
        src = slice(pair * LANES, (pair + 1) * LANES)
        dst = slice(2 * pair * LANES, (2 * pair + 1) * LANES)
        one = slice((2 * pair + 1) * LANES, (2 * pair + 2) * LANES)
        vv_scr[:, :blk, dst] = vp_ref[:, :, src]
        vv_scr[:, blk:, dst] = vc_ref[:, :, src]
        vv_scr[:, :, one] = jnp.ones(vv_scr.shape[:2] + (LANES,), BF16)

    r = lax.broadcasted_iota(jnp.int32, (blk, 2 * blk), 0)
    cidx = lax.broadcasted_iota(jnp.int32, (blk, 2 * blk), 1)
    dist = r + blk - cidx
    band = (dist >= 0) & (dist <= blk)
    lane = lax.broadcasted_iota(jnp.int32, (blk, LANES), 1)
    low = lane < DIL_DH

    for ri in range(nr):
        for j in range(nb):
            rows = slice(j * blk, (j + 1) * blk)
            band_rows = slice(j * blk, (j + 2) * blk)
            if j == 0:
                mask = band & ((cidx >= blk) | (chunk > 0))
            else:
                mask = band
            for pair in range(DIL_HPG // 2):
                cols = slice(pair * LANES, (pair + 1) * LANES)
                q2 = q_ref[ri, rows, cols]
                k2 = kk_scr[ri, band_rows, cols]
                v2 = vv_scr[ri, band_rows, 2 * pair * LANES:(2 * pair + 2) * LANES]
                pvs, ms = [], []
                for keep in (low, ~low):
                    qh = jnp.where(keep, q2, jnp.zeros_like(q2))
                    s = lax.dot_general(qh, k2, _NT, preferred_element_type=F32)
                    s = jnp.where(mask, s, NEG_BIG)
                    m = jnp.max(s, axis=-1, keepdims=True)
                    p = jnp.exp2(s - m)
                    pvs.append(jnp.dot(p.astype(BF16), v2, preferred_element_type=F32))
                    ms.append(m)
                num = jnp.where(low, pvs[0][:, :LANES], pvs[1][:, :LANES])
                denom = jnp.where(low, pvs[0][:, LANES:], pvs[1][:, LANES:])
                o_ref[ri, rows, cols] = (num / denom).astype(BF16)
                lse_ref[ri, rows, cols] = (jnp.where(low, ms[0], ms[1]) + jnp.log2(denom)) * LN2


def _dil_attn(qkv, group):
    batch, d, sub_len, _ = qkv.shape
    nb = DIL_BLOCKS_PER_STEP[group]
    nr = DIL_SUBSEQ_PER_STEP[group]
    blk = DIL_BLOCK
    rows = nb * blk
    cur = lambda part: pl.BlockSpec((None, nr, rows, DIL_WIDTH),
                                    lambda b, r, n: (b, r, n, part))
    prev = lambda part: pl.BlockSpec((None, nr, blk, DIL_WIDTH),
                                     lambda b, r, n: (b, r, jnp.maximum(n * nb - 1, 0), part))
    out = pl.BlockSpec((None, nr, rows, DIL_WIDTH), lambda b, r, n: (b, r, n, 0))
    return pl.pallas_call(
        functools.partial(_dil_attn_kernel, nb=nb, nr=nr),
        grid=(batch, d // nr, sub_len // rows),
        in_specs=[cur(0), prev(1), cur(1), prev(2), cur(2)],
        out_specs=[out, out],
        out_shape=[
            jax.ShapeDtypeStruct((batch, d, sub_len, DIL_WIDTH), BF16),
            jax.ShapeDtypeStruct((batch, d, sub_len, DIL_WIDTH), F32),
        ],
        scratch_shapes=[pltpu.VMEM((nr, rows + blk, DIL_WIDTH), BF16),
                        pltpu.VMEM((nr, rows + blk, 2 * DIL_WIDTH), BF16)],
        compiler_params=_params(3),
        name=f"dil_attn_{group}",
    )(qkv, qkv, qkv, qkv, qkv)


def _out_kernel(x_ref, ymla_ref, o0_ref, o1_ref, o2_ref, l0_ref, l1_ref, l2_ref,
                z_ref, gate_ref, wpm_ref, wpd_ref, wo_ref, pg_ref, out_ref, perm_scr, *, tm):
    n_sl = DIL_WIDTH // LANES

    def to_token_order(ref, d, slot):
        rows = tm // d
        for r in range(d):
            for s in range(n_sl):
                perm_scr[slot * n_sl + s, pl.ds(r, rows, stride=d), :] = (
                    ref[r, :, s * LANES:(s + 1) * LANES].astype(F32))

    def token_rows(slot, rs):
        return jnp.concatenate([perm_scr[slot * n_sl + s, rs, :] for s in range(n_sl)], axis=1)

    d1, d2 = DIL_DILATIONS[1], DIL_DILATIONS[2]
    to_token_order(o1_ref, d1, 0)
    to_token_order(l1_ref, d1, 1)
    to_token_order(o2_ref, d2, 2)
    to_token_order(l2_ref, d2, 3)

    for r0 in range(0, tm, tm // 2):
        rs = slice(r0, r0 + tm // 2)
        o0, l0 = o0_ref[0, rs, :].astype(F32), l0_ref[0, rs, :]
        o1, l1 = token_rows(0, rs), token_rows(1, rs)
        o2, l2 = token_rows(2, rs), token_rows(3, rs)
        m = jnp.maximum(jnp.maximum(l0, l1), l2)
        e0, e1, e2 = jnp.exp(l0 - m), jnp.exp(l1 - m), jnp.exp(l2 - m)
        y_dil = (e0 * o0 + e1 * o1 + e2 * o2) / (e0 + e1 + e2)

        a = (ymla_ref[rs, :].astype(F32) * z_ref[rs, :MLA_WIDTH].astype(F32)).astype(BF16)
        b = (y_dil * z_ref[rs, MLA_WIDTH:].astype(F32)).astype(BF16)
        pa = jnp.dot(a, wpm_ref[...], preferred_element_type=F32)
        pb = jnp.dot(b, wpd_ref[...], preferred_element_type=F32)
        merged = (gate_ref[rs, :D_MODEL].astype(F32) * pa
                  + gate_ref[rs, D_MODEL:].astype(F32) * pb)
        u = jnp.dot(merged.astype(BF16), wo_ref[...], preferred_element_type=F32)
        out_ref[rs, :] = x_ref[rs, :] + _rms(u, pg_ref[...])


def _out_block(x2, y_mla, o_dil, lse_dil, z, gate, wpm, wpd, wo, pg, seq, *, tm=512):
    t = x2.shape[0]
    tiles_per_seq = seq // tm
    row = lambda i: (i, 0)
    const = lambda i: (0, 0)
    rows = lambda w: pl.BlockSpec((tm, w), row)
    sub = lambda d: pl.BlockSpec((None, d, tm // d, DIL_WIDTH),
                                 lambda i: (i // tiles_per_seq, 0, i % tiles_per_seq, 0))
    subs = [sub(d) for d in DIL_DILATIONS]
    return pl.pallas_call(
        functools.partial(_out_kernel, tm=tm),
        grid=(t // tm,),
        in_specs=[rows(D_MODEL), rows(MLA_WIDTH)] + subs + subs
        + [rows(Z_WIDTH), rows(G_WIDTH),
           pl.BlockSpec((MLA_WIDTH, D_MODEL), const),
           pl.BlockSpec((DIL_WIDTH, D_MODEL), const),
           pl.BlockSpec((D_MODEL, D_MODEL), const),
           pl.BlockSpec((1, D_MODEL), const)],
        out_specs=rows(D_MODEL),
        out_shape=jax.ShapeDtypeStruct((t, D_MODEL), F32),
        scratch_shapes=[pltpu.VMEM((4 * DIL_WIDTH // LANES, tm, LANES), F32)],
        compiler_params=_params(1),
        name="out_block",
    )(x2, y_mla, *o_dil, *lse_dil, z, gate, wpm, wpd, wo, pg)


def _inv_freq(rot_dim):
    return ROPE_THETA ** (-jnp.arange(0, rot_dim, 2, dtype=F32) / rot_dim)


def _lane_freqs():
    f = jnp.zeros((LANES,), F32)
    f = f.at[:DIL_ROPE].set(jnp.tile(_inv_freq(DIL_ROPE), 2))
    f = f.at[MLA_NOPE:MLA_NOPE + MLA_ROPE].set(jnp.tile(_inv_freq(MLA_ROPE), 2))
    return f[None, :]


def _w_cat_kernel(w_ref, o_ref):
    lat = MLA_Q_RANK + MLA_KV_RANK
    cols = w_ref.shape[1]
    o_ref[:lat, :] = w_ref[:lat, :].astype(BF16)
    o_ref[lat:lat + MLA_NOPE, :] = jnp.zeros((MLA_NOPE, cols), BF16)
    o_ref[lat + MLA_NOPE:lat + MLA_NOPE + MLA_ROPE, :] = w_ref[lat:lat + MLA_ROPE, :].astype(BF16)
    o_ref[lat + MLA_NOPE + MLA_ROPE:LAT_WIDTH, :] = jnp.zeros(
        (LANES - MLA_NOPE - MLA_ROPE, cols), BF16)
    o_ref[LAT_WIDTH:, :] = w_ref[lat + MLA_ROPE:, :].astype(BF16)


def _w_cat(w_in_t, layer, *, cols=128):
    _, n, k = w_in_t.shape
    return pl.pallas_call(
        _w_cat_kernel,
        grid=(k // cols,),
        in_specs=[pl.BlockSpec((None, n, cols), lambda i: (layer, 0, i))],
        out_specs=pl.BlockSpec((IN_TOTAL, cols), lambda i: (0, i)),
        out_shape=jax.ShapeDtypeStruct((IN_TOTAL, k), BF16),
        compiler_params=_params(1),
        name="w_cat",
    )(w_in_t)


def _layer_weights(w_in, layer, w_uq, w_ukv):
    w_cat = _w_cat(jnp.swapaxes(w_in, 1, 2), layer)

    wq = w_uq.reshape(MLA_Q_RANK, MLA_HEADS, MLA_NOPE + MLA_ROPE)
    wq = jnp.pad(wq, ((0, 0), (0, 0), (0, MLA_HEAD_LANES - MLA_NOPE - MLA_ROPE)))
    wq = wq.reshape(MLA_Q_RANK, MLA_HEADS * MLA_HEAD_LANES).astype(BF16)
    return w_cat, wq, w_ukv.astype(BF16)


def kernel(x, positions, pre_norm_g, w_in, q_norm_g, w_uq, kv_norm_g, w_ukv,
           w_proj_mla, w_proj_dil, w_out, post_norm_g):
    batch, seq, _ = x.shape
    depth = w_in.shape[0]
    pos_col = positions.reshape(batch * seq, 1)
    freqs = _lane_freqs()
    x2 = x.reshape(batch * seq, D_MODEL)
    for layer in range(depth):
        w_cat, wq, wkv = _layer_weights(w_in, layer, w_uq[layer], w_ukv[layer])
        q, k, v, qkv0, qkv1, qkv2, z, gate = _in_proj(
            x2, pos_col, pre_norm_g[layer][None, :], w_cat, freqs,
            q_norm_g[layer][None, :], kv_norm_g[layer][None, :], wq, wkv, batch, seq)
        y_mla = _mla_attn(q, k, v, batch, seq)
        dil = [_dil_attn(qkv, g) for g, qkv in enumerate((qkv0, qkv1, qkv2))]
        x2 = _out_block(x2, y_mla, [o for o, _ in dil], [l for _, l in dil], z, gate,
                        w_proj_mla[layer].astype(BF16), w_proj_dil[layer].astype(BF16),
                        w_out[layer].astype(BF16), post_norm_g[layer][None, :], seq)
    return x2.reshape(batch, seq, D_MODEL)
```

```python
import functools
import math

import jax
import jax.numpy as jnp
from jax import lax
from jax.experimental import pallas as pl
from jax.experimental.pallas import tpu as pltpu

F32 = jnp.float32
BF16 = jnp.bfloat16

D_MODEL = 1024
ROPE_THETA = 500000.0
NORM_EPS = 1e-6

MLA_HEADS = 8
MLA_Q_RANK = 384
MLA_KV_RANK = 256
MLA_NOPE = 64
MLA_ROPE = 32
MLA_V = 64
MLA_WIDTH = MLA_HEADS * MLA_V
MLA_HEAD_LANES = 128
MLA_ROW_CHUNK = 64

DIL_DILATIONS = (1, 4, 16)
DIL_GROUPS = 3
DIL_HPG = 8
DIL_DH = 64
DIL_ROPE = 16
DIL_WIDTH = DIL_HPG * DIL_DH
DIL_BLOCK = 128
DIL_BLOCKS_PER_STEP = (8, 8, 2)
DIL_SUBSEQ_PER_STEP = (1, 1, 4)

LANES = 128
NEG_BIG = -1e30
LOG2E = math.log2(math.e)
LN2 = math.log(2.0)

VMEM_LIMIT = 56 * 1024 * 1024

LAT_WIDTH = MLA_Q_RANK + MLA_KV_RANK + LANES
QKV_WIDTH = 3 * DIL_WIDTH
Z_WIDTH = MLA_WIDTH + DIL_WIDTH
G_WIDTH = 2 * D_MODEL
IN_TOTAL = LAT_WIDTH + DIL_GROUPS * QKV_WIDTH + Z_WIDTH + G_WIDTH
N_SLABS = D_MODEL // LANES


def _params(n_axes):
    return pltpu.CompilerParams(
        dimension_semantics=("arbitrary",) * n_axes, vmem_limit_bytes=VMEM_LIMIT)


def _trig_tables(pos_col, freq_row):
    ang = pos_col.astype(F32) * freq_row
    return jnp.cos(ang), jnp.sin(ang)


def _rope_tables(cos, sin, lane_lo, width, period):
    lane = lax.broadcasted_iota(jnp.int32, cos.shape, 1) % period
    in_a = (lane >= lane_lo) & (lane < lane_lo + width)
    in_b = (lane >= lane_lo + width) & (lane < lane_lo + 2 * width)
    c = jnp.where(in_a | in_b, cos, 1.0)
    return c, jnp.where(in_a, -sin, 0.0), jnp.where(in_b, sin, 0.0)


def _apply_rope(a, c, sa, sb, shift):
    return a * c + pltpu.roll(a, LANES - shift, 1) * sa + pltpu.roll(a, shift, 1) * sb


def _rms(x, g):
    ms = jnp.mean(x * x, axis=-1, keepdims=True)
    return x * lax.rsqrt(ms + NORM_EPS) * g


def _sigmoid(x):
    return 1.0 / (1.0 + jnp.exp(-x))


def _in_proj_kernel(x_ref, pos_ref, g_ref, w_ref, freq_ref, qg_ref, kvg_ref,
                    wq_ref, wkv_ref,
                    q_ref, k_ref, v_ref, qkv0_ref, qkv1_ref, qkv2_ref, z_ref, gate_ref,
                    h_scr, tab_scr, hp_scr, *, tm):
    h = _rms(x_ref[...], g_ref[...])
    hb = h.astype(BF16)
    cos, sin = _trig_tables(pos_ref[...], freq_ref[...])

    def proj(lhs, c0, width):
        return lax.dot_general(lhs, w_ref[c0:c0 + width, :], _NT, preferred_element_type=F32)

    lat = proj(hb, 0, LAT_WIDTH)
    cq = _rms(lat[:, :MLA_Q_RANK], qg_ref[...]).astype(BF16)
    ckv = _rms(lat[:, MLA_Q_RANK:MLA_Q_RANK + MLA_KV_RANK], kvg_ref[...]).astype(BF16)
    mhalf = MLA_ROPE // 2
    mtabs = _rope_tables(cos, sin, MLA_NOPE, mhalf, LANES)
    kr = _apply_rope(lat[:, MLA_Q_RANK + MLA_KV_RANK:], *mtabs, mhalf)
    q_scale = (MLA_NOPE + MLA_ROPE) ** -0.5 * LOG2E
    lane = lax.broadcasted_iota(jnp.int32, kr.shape, 1)
    q_all = jnp.dot(cq, wq_ref[...], preferred_element_type=F32)
    kv_all = jnp.dot(ckv, wkv_ref[...], preferred_element_type=F32)
    for hd in range(MLA_HEADS):
        sl = slice(hd * MLA_HEAD_LANES, (hd + 1) * MLA_HEAD_LANES)
        q_ref[:, sl] = (_apply_rope(q_all[:, sl], *mtabs, mhalf) * q_scale).astype(BF16)
        kv = kv_all[:, sl]
        k_ref[:, sl] = jnp.where(lane < MLA_NOPE, kv, kr).astype(BF16)
        v_ref[:, sl] = jnp.where(lane == MLA_V, 1.0, pltpu.roll(kv, MLA_NOPE, 1)).astype(BF16)

    dhalf = DIL_ROPE // 2
    first = lax.broadcasted_iota(jnp.int32, cos.shape, 1) < DIL_DH
    dtabs = _rope_tables(jnp.where(first, cos, pltpu.roll(cos, DIL_DH, 1)),
                         jnp.where(first, sin, pltpu.roll(sin, DIL_DH, 1)), 0, dhalf, DIL_DH)
    d_scale = DIL_DH ** -0.5 * LOG2E

    def qkv_pieces(lhs, group, tables):
        for part in range(3):
            acc = proj(lhs, LAT_WIDTH + (part * DIL_GROUPS + group) * DIL_WIDTH, DIL_WIDTH)
            for l0 in range(0, DIL_WIDTH, LANES):
                a = acc[:, l0:l0 + LANES]
                if part < 2:
                    a = _apply_rope(a, *tables, dhalf)
                if part == 0:
                    a = a * d_scale
                yield part * DIL_WIDTH + l0, a.astype(BF16)

    for off, piece in qkv_pieces(hb, 0, dtabs):
        qkv0_ref[0, :, off:off + LANES] = piece

    for s in range(N_SLABS):
        h_scr[s] = h[:, s * LANES:(s + 1) * LANES]
    for i, t in enumerate(dtabs):
        tab_scr[i] = t
    for group, out_ref in ((1, qkv1_ref), (2, qkv2_ref)):
        d = DIL_DILATIONS[group]
        rows = tm // d
        for r in range(d):
            for s in range(N_SLABS):
                hp_scr[r * rows:(r + 1) * rows, s * LANES:(s + 1) * LANES] = (
                    h_scr[s, pl.ds(r, rows, stride=d), :].astype(BF16))
        ptabs = tuple(
            jnp.concatenate([tab_scr[i, pl.ds(r, rows, stride=d), :] for r in range(d)], axis=0)
            for i in range(3))
        for off, piece in qkv_pieces(hp_scr[...], group, ptabs):
            for r in range(d):
                out_ref[r, :, off:off + LANES] = piece[r * rows:(r + 1) * rows, :]

    col = LAT_WIDTH + DIL_GROUPS * QKV_WIDTH
    for c0 in range(0, Z_WIDTH, 512):
        zacc = proj(hb, col + c0, 512)
        z_ref[:, c0:c0 + 512] = (zacc * _sigmoid(zacc)).astype(BF16)
    col += Z_WIDTH
    for c0 in range(0, G_WIDTH, 512):
        gate_ref[:, c0:c0 + 512] = _sigmoid(proj(hb, col + c0, 512)).astype(BF16)


def _in_proj(x2, pos_col, g_pre, w_cat, freqs, qg, kvg, wq, wkv, batch, seq, *, tm=512):
    t = x2.shape[0]
    tiles_per_seq = seq // tm
    mla_width = MLA_HEADS * MLA_HEAD_LANES
    row = lambda i: (i, 0)
    const = lambda i: (0, 0)
    resident = lambda shape: pl.BlockSpec(shape, const, pipeline_mode=pl.Buffered(1))

    def qkv_spec(d):
        return pl.BlockSpec((None, d, tm // d, QKV_WIDTH),
                            lambda i: (i // tiles_per_seq, 0, i % tiles_per_seq, 0))

    return pl.pallas_call(
        functools.partial(_in_proj_kernel, tm=tm),
        grid=(t // tm,),
        in_specs=[
            pl.BlockSpec((tm, D_MODEL), row),
            pl.BlockSpec((tm, 1), row),
            resident((1, D_MODEL)),
            resident((IN_TOTAL, D_MODEL)),
            resident((1, LANES)),
            resident((1, MLA_Q_RANK)),
            resident((1, MLA_KV_RANK)),
            resident((MLA_Q_RANK, mla_width)),
            resident((MLA_KV_RANK, mla_width)),
        ],
        out_specs=[pl.BlockSpec((tm, mla_width), row)] * 3
        + [qkv_spec(d) for d in DIL_DILATIONS]
        + [pl.BlockSpec((tm, Z_WIDTH), row), pl.BlockSpec((tm, G_WIDTH), row)],
        out_shape=[jax.ShapeDtypeStruct((t, mla_width), BF16)] * 3
        + [jax.ShapeDtypeStruct((batch, d, seq // d, QKV_WIDTH), BF16) for d in DIL_DILATIONS]
        + [jax.ShapeDtypeStruct((t, Z_WIDTH), BF16), jax.ShapeDtypeStruct((t, G_WIDTH), BF16)],
        scratch_shapes=[
            pltpu.VMEM((N_SLABS, tm, LANES), F32),
            pltpu.VMEM((3, tm, LANES), F32),
            pltpu.VMEM((tm, D_MODEL), BF16),
        ],
        compiler_params=_params(1),
        name="in_proj",
    )(x2, pos_col, g_pre, w_cat, freqs, qg, kvg, wq, wkv)


_NT = (((1,), (1,)), ((), ()))


def _mla_attn_kernel(q_ref, k_ref, v_ref, o_ref, s_scr, p_scr, m_scr, alpha_scr, acc_scr, *,
                     tk, heads_per_step):
    qi = pl.program_id(2)
    tq = 2 * tk
    slices = [slice(hd * MLA_HEAD_LANES, (hd + 1) * MLA_HEAD_LANES)
              for hd in range(heads_per_step)]
    every = slice(0, tq)
    upper, lower = slice(0, tk), slice(tk, tq)

    def scores(j, slot, rows=every):
        start = pl.multiple_of(j * tk, tk)
        for hd, sl in enumerate(slices):
            s_scr[slot, hd, rows, :] = lax.dot_general(
                q_ref[rows, sl], k_ref[pl.ds(start, tk), sl], _NT, preferred_element_type=F32)

    def update(j, slot, rows=every, masked=False):
        start = pl.multiple_of(j * tk, tk)
        for hd, sl in enumerate(slices):
            for c0 in range(rows.start, rows.stop, MLA_ROW_CHUNK):
                rc = slice(c0, c0 + MLA_ROW_CHUNK)
                s = s_scr[slot, hd, rc, :]
                if masked:
                    r = lax.broadcasted_iota(jnp.int32, s.shape, 0) + (c0 - rows.start)
                    cidx = lax.broadcasted_iota(jnp.int32, s.shape, 1)
                    s = jnp.where(cidx <= r, s, NEG_BIG)
                m = m_scr[hd, rc, :]
                m_new = jnp.maximum(m, jnp.max(s, axis=-1, keepdims=True))
                p_scr[hd, rc, :] = jnp.exp2(s - m_new).astype(BF16)
                alpha_scr[hd, rc, :] = jnp.exp2(m - m_new)
                m_scr[hd, rc, :] = m_new
            acc_scr[hd, rows, :] = alpha_scr[hd, rows, :] * acc_scr[hd, rows, :] + jnp.dot(
                p_scr[hd, rows, :], v_ref[pl.ds(start, tk), sl], preferred_element_type=F32)

    m_scr[...] = jnp.full(m_scr.shape, NEG_BIG, F32)
    acc_scr[...] = jnp.zeros(acc_scr.shape, F32)
    scores(0, 0)

    def pair(jj, carry):
        j0 = 2 * jj
        scores(j0 + 1, 1)
        update(j0, 0)
        scores(j0 + 2, 0)
        update(j0 + 1, 1)
        return carry

    lax.fori_loop(0, qi, pair, 0)

    jd = 2 * qi
    scores(jd + 1, 1, lower)
    update(jd, 0, upper, masked=True)
    update(jd, 0, lower)
    update(jd + 1, 1, lower, masked=True)
    for hd in range(heads_per_step):
        acc = acc_scr[hd]
        denom = acc[:, MLA_V:MLA_V + 1]
        o_ref[:, hd * MLA_V:(hd + 1) * MLA_V] = (acc[:, :MLA_V] / denom).astype(BF16)


def _mla_attn(q, k, v, batch, seq, *, tk=512, heads_per_step=2):
    width = MLA_HEADS * MLA_HEAD_LANES
    tq = 2 * tk
    q3, k3, v3 = (a.reshape(batch, seq, width) for a in (q, k, v))
    hw = heads_per_step * MLA_HEAD_LANES
    out = pl.pallas_call(
        functools.partial(_mla_attn_kernel, tk=tk, heads_per_step=heads_per_step),
        grid=(batch, MLA_HEADS // heads_per_step, seq // tq),
        in_specs=[
            pl.BlockSpec((None, tq, hw), lambda b, hp, i: (b, i, hp)),
            pl.BlockSpec((None, seq, hw), lambda b, hp, i: (b, 0, hp)),
            pl.BlockSpec((None, seq, hw), lambda b, hp, i: (b, 0, hp)),
        ],
        out_specs=pl.BlockSpec((None, tq, heads_per_step * MLA_V), lambda b, hp, i: (b, i, hp)),
        out_shape=jax.ShapeDtypeStruct((batch, seq, MLA_WIDTH), BF16),
        scratch_shapes=[
            pltpu.VMEM((2, heads_per_step, tq, tk), F32),
            pltpu.VMEM((heads_per_step, tq, tk), BF16),
            pltpu.VMEM((heads_per_step, tq, 1), F32),
            pltpu.VMEM((heads_per_step, tq, 1), F32),
            pltpu.VMEM((heads_per_step, tq, MLA_HEAD_LANES), F32),
        ],
        compiler_params=_params(3),
        name="mla_attn",
    )(q3, k3, v3)
    return out.reshape(batch * seq, MLA_WIDTH)


def _dil_attn_kernel(q_ref, kp_ref, kc_ref, vp_ref, vc_ref, o_ref, lse_ref,
                     kk_scr, vv_scr, *, nb, nr):
    chunk = pl.program_id(2)
    blk = DIL_BLOCK
    kk_scr[:, :blk, :] = kp_ref[...]
    kk_scr[:, blk:, :] = kc_ref[...]
    for pair in range(DIL_HPG // 2):
        src = slice(pair * LANES, (pair + 1) * LANES)
        dst = slice(2 * pair * LANES, (2 * pair + 1) * LANES)
        one = slice((2 * pair + 1) * LANES, (2 * pair + 2) * LANES)
        vv_scr[:, :blk, dst] = vp_ref[:, :, src]
        vv_scr[:, blk:, dst] = vc_ref[:, :, src]
        vv_scr[:, :, one] = jnp.ones(vv_scr.shape[:2] + (LANES,), BF16)

    r = lax.broadcasted_iota(jnp.int32, (blk, 2 * blk), 0)
    cidx = lax.broadcasted_iota(jnp.int32, (blk, 2 * blk), 1)
    dist = r + blk - cidx
    band = jnp.where((dist >= 0) & (dist <= blk), 0.0, NEG_BIG)
    first_band = jnp.where((cidx >= blk) | (chunk > 0), band, NEG_BIG)
    lane = lax.broadcasted_iota(jnp.int32, (blk, LANES), 1)
    low = lane < DIL_DH

    for ri in range(nr):
        for j in range(nb):
            rows = slice(j * blk, (j + 1) * blk)
            band_rows = slice(j * blk, (j + 2) * blk)
            bias = first_band if j == 0 else band
            for pair in range(DIL_HPG // 2):
                cols = slice(pair * LANES, (pair + 1) * LANES)
                q2 = q_ref[ri, rows, cols]
                k2 = kk_scr[ri, band_rows, cols]
                v2 = vv_scr[ri, band_rows, 2 * pair * LANES:(2 * pair + 2) * LANES]
                qs = jnp.concatenate([jnp.where(low, q2, jnp.zeros_like(q2)),
                                      jnp.where(low, jnp.zeros_like(q2), q2)], axis=0)
                s = lax.dot_general(qs, k2, _NT, preferred_element_type=F32)
                s = s + jnp.concatenate([bias, bias], axis=0)
                m = jnp.max(s, axis=-1, keepdims=True)
                p = jnp.exp2(s - m)
                pv = jnp.dot(p.astype(BF16), v2, preferred_element_type=F32)
                num = jnp.where(low, pv[:blk, :LANES], pv[blk:, :LANES])
                denom = jnp.where(low, pv[:blk, LANES:], pv[blk:, LANES:])
                m_sel = jnp.where(low, m[:blk], m[blk:])
                o_ref[ri, rows, cols] = (num / denom).astype(BF16)
                lse_ref[ri, rows, cols] = (m_sel + jnp.log2(denom)) * LN2


def _dil_attn(qkv, group):
    batch, d, sub_len, _ = qkv.shape
    nb = DIL_BLOCKS_PER_STEP[group]
    nr = DIL_SUBSEQ_PER_STEP[group]
    blk = DIL_BLOCK
    rows = nb * blk
    cur = lambda part: pl.BlockSpec((None, nr, rows, DIL_WIDTH),
                                    lambda b, r, n: (b, r, n, part))
    prev = lambda part: pl.BlockSpec((None, nr, blk, DIL_WIDTH),
                                     lambda b, r, n: (b, r, jnp.maximum(n * nb - 1, 0), part))
    out = pl.BlockSpec((None, nr, rows, DIL_WIDTH), lambda b, r, n: (b, r, n, 0))
    return pl.pallas_call(
        functools.partial(_dil_attn_kernel, nb=nb, nr=nr),
        grid=(batch, d // nr, sub_len // rows),
        in_specs=[cur(0), prev(1), cur(1), prev(2), cur(2)],
        out_specs=[out, out],
        out_shape=[
            jax.ShapeDtypeStruct((batch, d, sub_len, DIL_WIDTH), BF16),
            jax.ShapeDtypeStruct((batch, d, sub_len, DIL_WIDTH), F32),
        ],
        scratch_shapes=[pltpu.VMEM((nr, rows + blk, DIL_WIDTH), BF16),
                        pltpu.VMEM((nr, rows + blk, 2 * DIL_WIDTH), BF16)],
        compiler_params=_params(3),
        name=f"dil_attn_{group}",
    )(qkv, qkv, qkv, qkv, qkv)


def _out_kernel(x_ref, ymla_ref, o0_ref, o1_ref, o2_ref, l0_ref, l1_ref, l2_ref,
                z_ref, gate_ref, wpm_ref, wpd_ref, wo_ref, pg_ref, out_ref, perm_scr, *, tm):
    n_sl = DIL_WIDTH // LANES

    def to_token_order(ref, d, slot):
        rows = tm // d
        for r in range(d):
            for s in range(n_sl):
                perm_scr[slot * n_sl + s, pl.ds(r, rows, stride=d), :] = (
                    ref[r, :, s * LANES:(s + 1) * LANES].astype(F32))

    def token_rows(slot, rs):
        return jnp.concatenate([perm_scr[slot * n_sl + s, rs, :] for s in range(n_sl)], axis=1)

    d1, d2 = DIL_DILATIONS[1], DIL_DILATIONS[2]
    to_token_order(o1_ref, d1, 0)
    to_token_order(l1_ref, d1, 1)
    to_token_order(o2_ref, d2, 2)
    to_token_order(l2_ref, d2, 3)

    for r0 in range(0, tm, tm // 2):
        rs = slice(r0, r0 + tm // 2)
        o0, l0 = o0_ref[0, rs, :].astype(F32), l0_ref[0, rs, :]
        o1, l1 = token_rows(0, rs), token_rows(1, rs)
        o2, l2 = token_rows(2, rs), token_rows(3, rs)
        m = jnp.maximum(jnp.maximum(l0, l1), l2)
        e0, e1, e2 = jnp.exp(l0 - m), jnp.exp(l1 - m), jnp.exp(l2 - m)
        y_dil = (e0 * o0 + e1 * o1 + e2 * o2) / (e0 + e1 + e2)

        a = (ymla_ref[rs, :].astype(F32) * z_ref[rs, :MLA_WIDTH].astype(F32)).astype(BF16)
        b = (y_dil * z_ref[rs, MLA_WIDTH:].astype(F32)).astype(BF16)
        pa = jnp.dot(a, wpm_ref[...], preferred_element_type=F32)
        pb = jnp.dot(b, wpd_ref[...], preferred_element_type=F32)
        merged = (gate_ref[rs, :D_MODEL].astype(F32) * pa
                  + gate_ref[rs, D_MODEL:].astype(F32) * pb)
        u = jnp.dot(merged.astype(BF16), wo_ref[...], preferred_element_type=F32)
        out_ref[rs, :] = x_ref[rs, :] + _rms(u, pg_ref[...])


def _out_block(x2, y_mla, o_dil, lse_dil, z, gate, wpm, wpd, wo, pg, seq, *, tm=512):
    t = x2.shape[0]
    tiles_per_seq = seq // tm
    row = lambda i: (i, 0)
    const = lambda i: (0, 0)
    rows = lambda w: pl.BlockSpec((tm, w), row)
    sub = lambda d: pl.BlockSpec((None, d, tm // d, DIL_WIDTH),
                                 lambda i: (i // tiles_per_seq, 0, i % tiles_per_seq, 0))
    subs = [sub(d) for d in DIL_DILATIONS]
    return pl.pallas_call(
        functools.partial(_out_kernel, tm=tm),
        grid=(t // tm,),
        in_specs=[rows(D_MODEL), rows(MLA_WIDTH)] + subs + subs
        + [rows(Z_WIDTH), rows(G_WIDTH),
           pl.BlockSpec((MLA_WIDTH, D_MODEL), const),
           pl.BlockSpec((DIL_WIDTH, D_MODEL), const),
           pl.BlockSpec((D_MODEL, D_MODEL), const),
           pl.BlockSpec((1, D_MODEL), const)],
        out_specs=rows(D_MODEL),
        out_shape=jax.ShapeDtypeStruct((t, D_MODEL), F32),
        scratch_shapes=[pltpu.VMEM((4 * DIL_WIDTH // LANES, tm, LANES), F32)],
        compiler_params=_params(1),
        name="out_block",
    )(x2, y_mla, *o_dil, *lse_dil, z, gate, wpm, wpd, wo, pg)


def _inv_freq(rot_dim):
    return ROPE_THETA ** (-jnp.arange(0, rot_dim, 2, dtype=F32) / rot_dim)


def _lane_freqs():
    f = jnp.zeros((LANES,), F32)
    f = f.at[:DIL_ROPE].set(jnp.tile(_inv_freq(DIL_ROPE), 2))
    f = f.at[MLA_NOPE:MLA_NOPE + MLA_ROPE].set(jnp.tile(_inv_freq(MLA_ROPE), 2))
    return f[None, :]


def _w_cat_kernel(w_ref, o_ref):
    lat = MLA_Q_RANK + MLA_KV_RANK
    cols = w_ref.shape[1]
    o_ref[:lat, :] = w_ref[:lat, :].astype(BF16)
    o_ref[lat:lat + MLA_NOPE, :] = jnp.zeros((MLA_NOPE, cols), BF16)
    o_ref[lat + MLA_NOPE:lat + MLA_NOPE + MLA_ROPE, :] = w_ref[lat:lat + MLA_ROPE, :].astype(BF16)
    o_ref[lat + MLA_NOPE + MLA_ROPE:LAT_WIDTH, :] = jnp.zeros(
        (LANES - MLA_NOPE - MLA_ROPE, cols), BF16)
    o_ref[LAT_WIDTH:, :] = w_ref[lat + MLA_ROPE:, :].astype(BF16)


def _w_cat(w_in_t, layer, *, cols=128):
    _, n, k = w_in_t.shape
    return pl.pallas_call(
        _w_cat_kernel,
        grid=(k // cols,),
        in_specs=[pl.BlockSpec((None, n, cols), lambda i: (layer, 0, i))],
        out_specs=pl.BlockSpec((IN_TOTAL, cols), lambda i: (0, i)),
        out_shape=jax.ShapeDtypeStruct((IN_TOTAL, k), BF16),
        compiler_params=_params(1),
        name="w_cat",
    )(w_in_t)


def _layer_weights(w_in, layer, w_uq, w_ukv):
    w_cat = _w_cat(jnp.swapaxes(w_in, 1, 2), layer)

    wq = w_uq.reshape(MLA_Q_RANK, MLA_HEADS, MLA_NOPE + MLA_ROPE)
    wq = jnp.pad(wq, ((0, 0), (0, 0), (0, MLA_HEAD_LANES - MLA_NOPE - MLA_ROPE)))
    wq = wq.reshape(MLA_Q_RANK, MLA_HEADS * MLA_HEAD_LANES).astype(BF16)
    return w_cat, wq, w_ukv.astype(BF16)


def kernel(x, positions, pre_norm_g, w_in, q_norm_g, w_uq, kv_norm_g, w_ukv,
           w_proj_mla, w_proj_dil, w_out, post_norm_g):
    batch, seq, _ = x.shape
    depth = w_in.shape[0]
    pos_col = positions.reshape(batch * seq, 1)
    freqs = _lane_freqs()
    x2 = x.reshape(batch * seq, D_MODEL)
    for layer in range(depth):
        w_cat, wq, wkv = _layer_weights(w_in, layer, w_uq[layer], w_ukv[layer])
        q, k, v, qkv0, qkv1, qkv2, z, gate = _in_proj(
            x2, pos_col, pre_norm_g[layer][None, :], w_cat, freqs,
            q_norm_g[layer][None, :], kv_norm_g[layer][None, :], wq, wkv, batch, seq)
        y_mla = _mla_attn(q, k, v, batch, seq)
        dil = [_dil_attn(qkv, g) for g, qkv in enumerate((qkv0, qkv1, qkv2))]
        x2 = _out_block(x2, y_mla, [o for o, _ in dil], [l for _, l in dil], z, gate,
                        w_proj_mla[layer].astype(BF16), w_proj_dil[layer].astype(BF16),
                        w_out[layer].astype(BF16), post_norm_g[layer][None, :], seq)
    return x2.reshape(batch, seq, D_MODEL)
```

```python
import functools
import math

import jax
import jax.numpy as jnp
from jax import lax
from jax.experimental import pallas as pl
from jax.experimental.pallas import tpu as pltpu

F32 = jnp.float32
BF16 = jnp.bfloat16

D_MODEL = 1024
ROPE_THETA = 500000.0
NORM_EPS = 1e-6

MLA_HEADS = 8
MLA_Q_RANK = 384
MLA_KV_RANK = 256
MLA_NOPE = 64
MLA_ROPE = 32
MLA_V = 64
MLA_WIDTH = MLA_HEADS * MLA_V
MLA_HEAD_LANES = 128
MLA_ROW_CHUNK = 64

DIL_DILATIONS = (1, 4, 16)
DIL_GROUPS = 3
DIL_HPG = 8
DIL_DH = 64
DIL_ROPE = 16
DIL_WIDTH = DIL_HPG * DIL_DH
DIL_BLOCK = 128
DIL_BLOCKS_PER_STEP = (8, 8, 2)
DIL_SUBSEQ_PER_STEP = (1, 1, 4)

LANES = 128
NEG_BIG = -1e30
LOG2E = math.log2(math.e)

VMEM_LIMIT = 56 * 1024 * 1024

LAT_WIDTH = MLA_Q_RANK + MLA_KV_RANK + LANES
QKV_WIDTH = 3 * DIL_WIDTH
Z_WIDTH = MLA_WIDTH + DIL_WIDTH
G_WIDTH = 2 * D_MODEL
IN_TOTAL = LAT_WIDTH + DIL_GROUPS * QKV_WIDTH + Z_WIDTH + G_WIDTH
N_SLABS = D_MODEL // LANES


def _params(n_axes):
    return pltpu.CompilerParams(
        dimension_semantics=("arbitrary",) * n_axes, vmem_limit_bytes=VMEM_LIMIT)


def _trig_tables(pos_col, freq_row):
    ang = pos_col.astype(F32) * freq_row
    return jnp.cos(ang), jnp.sin(ang)


def _rope_tables(cos, sin, lane_lo, width, period):
    lane = lax.broadcasted_iota(jnp.int32, cos.shape, 1) % period
    in_a = (lane >= lane_lo) & (lane < lane_lo + width)
    in_b = (lane >= lane_lo + width) & (lane < lane_lo + 2 * width)
    c = jnp.where(in_a | in_b, cos, 1.0)
    return c, jnp.where(in_a, -sin, 0.0), jnp.where(in_b, sin, 0.0)


def _apply_rope(a, c, sa, sb, shift):
    return a * c + pltpu.roll(a, LANES - shift, 1) * sa + pltpu.roll(a, shift, 1) * sb


def _rms(x, g):
    ms = jnp.mean(x * x, axis=-1, keepdims=True)
    return x * lax.rsqrt(ms + NORM_EPS) * g


def _sigmoid(x):
    return 1.0 / (1.0 + jnp.exp(-x))


def _in_proj_kernel(x_ref, pos_ref, g_ref, w_ref, freq_ref, qg_ref, kvg_ref,
                    wq_ref, wkv_ref,
                    q_ref, k_ref, v_ref, qkv0_ref, qkv1_ref, qkv2_ref, z_ref, gate_ref,
                    h_scr, tab_scr, hp_scr, *, tm):
    h = _rms(x_ref[...], g_ref[...])
    hb = h.astype(BF16)
    cos, sin = _trig_tables(pos_ref[...], freq_ref[...])

    def proj(lhs, c0, width):
        return lax.dot_general(lhs, w_ref[c0:c0 + width, :], _NT, preferred_element_type=F32)

    lat = proj(hb, 0, LAT_WIDTH)
    cq = _rms(lat[:, :MLA_Q_RANK], qg_ref[...]).astype(BF16)
    ckv = _rms(lat[:, MLA_Q_RANK:MLA_Q_RANK + MLA_KV_RANK], kvg_ref[...]).astype(BF16)
    mhalf = MLA_ROPE // 2
    mtabs = _rope_tables(cos, sin, MLA_NOPE, mhalf, LANES)
    kr = _apply_rope(lat[:, MLA_Q_RANK + MLA_KV_RANK:], *mtabs, mhalf)
    q_scale = (MLA_NOPE + MLA_ROPE) ** -0.5 * LOG2E
    lane = lax.broadcasted_iota(jnp.int32, kr.shape, 1)
    q_all = jnp.dot(cq, wq_ref[...], preferred_element_type=F32)
    kv_all = jnp.dot(ckv, wkv_ref[...], preferred_element_type=F32)
    for hd in range(MLA_HEADS):
        sl = slice(hd * MLA_HEAD_LANES, (hd + 1) * MLA_HEAD_LANES)
        q_ref[:, sl] = (_apply_rope(q_all[:, sl], *mtabs, mhalf) * q_scale).astype(BF16)
        kv = kv_all[:, sl]
        k_ref[:, sl] = jnp.where(lane < MLA_NOPE, kv, kr).astype(BF16)
        v_ref[:, sl] = jnp.where(lane == MLA_V, 1.0, pltpu.roll(kv, MLA_NOPE, 1)).astype(BF16)

    dhalf = DIL_ROPE // 2
    first = lax.broadcasted_iota(jnp.int32, cos.shape, 1) < DIL_DH
    dtabs = _rope_tables(jnp.where(first, cos, pltpu.roll(cos, DIL_DH, 1)),
                         jnp.where(first, sin, pltpu.roll(sin, DIL_DH, 1)), 0, dhalf, DIL_DH)
    d_scale = DIL_DH ** -0.5 * LOG2E

    def qkv_pieces(lhs, group, tables):
        for part in range(3):
            acc = proj(lhs, LAT_WIDTH + (part * DIL_GROUPS + group) * DIL_WIDTH, DIL_WIDTH)
            for l0 in range(0, DIL_WIDTH, LANES):
                a = acc[:, l0:l0 + LANES]
                if part < 2:
                    a = _apply_rope(a, *tables, dhalf)
                if part == 0:
                    a = a * d_scale
                yield part * DIL_WIDTH + l0, a.astype(BF16)

    for off, piece in qkv_pieces(hb, 0, dtabs):
        qkv0_ref[0, :, off:off + LANES] = piece

    for s in range(N_SLABS):
        h_scr[s] = h[:, s * LANES:(s + 1) * LANES]
    for i, t in enumerate(dtabs):
        tab_scr[i] = t
    for group, out_ref in ((1, qkv1_ref), (2, qkv2_ref)):
        d = DIL_DILATIONS[group]
        rows = tm // d
        for r in range(d):
            for s in range(N_SLABS):
                hp_scr[r * rows:(r + 1) * rows, s * LANES:(s + 1) * LANES] = (
                    h_scr[s, pl.ds(r, rows, stride=d), :].astype(BF16))
        ptabs = tuple(
            jnp.concatenate([tab_scr[i, pl.ds(r, rows, stride=d), :] for r in range(d)], axis=0)
            for i in range(3))
        for off, piece in qkv_pieces(hp_scr[...], group, ptabs):
            for r in range(d):
                out_ref[r, :, off:off + LANES] = piece[r * rows:(r + 1) * rows, :]

    col = LAT_WIDTH + DIL_GROUPS * QKV_WIDTH
    for c0 in range(0, Z_WIDTH, 512):
        zacc = proj(hb, col + c0, 512)
        z_ref[:, c0:c0 + 512] = (zacc * _sigmoid(zacc)).astype(BF16)
    col += Z_WIDTH
    for c0 in range(0, G_WIDTH, 512):
        gate_ref[:, c0:c0 + 512] = _sigmoid(proj(hb, col + c0, 512)).astype(BF16)


def _in_proj(x2, pos_col, g_pre, w_cat, freqs, qg, kvg, wq, wkv, batch, seq, *, tm=512):
    t = x2.shape[0]
    tiles_per_seq = seq // tm
    mla_width = MLA_HEADS * MLA_HEAD_LANES
    row = lambda i: (i, 0)
    const = lambda i: (0, 0)
    resident = lambda shape: pl.BlockSpec(shape, const, pipeline_mode=pl.Buffered(1))

    def qkv_spec(d):
        return pl.BlockSpec((None, d, tm // d, QKV_WIDTH),
                            lambda i: (i // tiles_per_seq, 0, i % tiles_per_seq, 0))

    return pl.pallas_call(
        functools.partial(_in_proj_kernel, tm=tm),
        grid=(t // tm,),
        in_specs=[
            pl.BlockSpec((tm, D_MODEL), row),
            pl.BlockSpec((tm, 1), row),
            resident((1, D_MODEL)),
            resident((IN_TOTAL, D_MODEL)),
            resident((1, LANES)),
            resident((1, MLA_Q_RANK)),
            resident((1, MLA_KV_RANK)),
            resident((MLA_Q_RANK, mla_width)),
            resident((MLA_KV_RANK, mla_width)),
        ],
        out_specs=[pl.BlockSpec((tm, mla_width), row)] * 3
        + [qkv_spec(d) for d in DIL_DILATIONS]
        + [pl.BlockSpec((tm, Z_WIDTH), row), pl.BlockSpec((tm, G_WIDTH), row)],
        out_shape=[jax.ShapeDtypeStruct((t, mla_width), BF16)] * 3
        + [jax.ShapeDtypeStruct((batch, d, seq // d, QKV_WIDTH), BF16) for d in DIL_DILATIONS]
        + [jax.ShapeDtypeStruct((t, Z_WIDTH), BF16), jax.ShapeDtypeStruct((t, G_WIDTH), BF16)],
        scratch_shapes=[
            pltpu.VMEM((N_SLABS, tm, LANES), F32),
            pltpu.VMEM((3, tm, LANES), F32),
            pltpu.VMEM((tm, D_MODEL), BF16),
        ],
        compiler_params=_params(1),
        name="in_proj",
    )(x2, pos_col, g_pre, w_cat, freqs, qg, kvg, wq, wkv)


_NT = (((1,), (1,)), ((), ()))


def _mla_attn_kernel(q_ref, k_ref, v_ref, o_ref, s_scr, p_scr, m_scr, alpha_scr, acc_scr, *,
                     tk, heads_per_step):
    qi = pl.program_id(2)
    tq = 2 * tk
    slices = [slice(hd * MLA_HEAD_LANES, (hd + 1) * MLA_HEAD_LANES)
              for hd in range(heads_per_step)]
    every = slice(0, tq)
    upper, lower = slice(0, tk), slice(tk, tq)

    def scores(j, slot, rows=every):
        start = pl.multiple_of(j * tk, tk)
        for hd, sl in enumerate(slices):
            s_scr[slot, hd, rows, :] = lax.dot_general(
                q_ref[rows, sl], k_ref[pl.ds(start, tk), sl], _NT, preferred_element_type=F32)

    def update(j, slot, rows=every, masked=False):
        start = pl.multiple_of(j * tk, tk)
        for hd, sl in enumerate(slices):
            for c0 in range(rows.start, rows.stop, MLA_ROW_CHUNK):
                rc = slice(c0, c0 + MLA_ROW_CHUNK)
                s = s_scr[slot, hd, rc, :]
                if masked:
                    r = lax.broadcasted_iota(jnp.int32, s.shape, 0) + (c0 - rows.start)
                    cidx = lax.broadcasted_iota(jnp.int32, s.shape, 1)
                    s = jnp.where(cidx <= r, s, NEG_BIG)
                m = m_scr[hd, rc, :]
                m_new = jnp.maximum(m, jnp.max(s, axis=-1, keepdims=True))
                p_scr[hd, rc, :] = jnp.exp2(s - m_new).astype(BF16)
                alpha_scr[hd, rc, :] = jnp.exp2(m - m_new)
                m_scr[hd, rc, :] = m_new
            acc_scr[hd, rows, :] = alpha_scr[hd, rows, :] * acc_scr[hd, rows, :] + jnp.dot(
                p_scr[hd, rows, :], v_ref[pl.ds(start, tk), sl], preferred_element_type=F32)

    m_scr[...] = jnp.full(m_scr.shape, NEG_BIG, F32)
    acc_scr[...] = jnp.zeros(acc_scr.shape, F32)
    scores(0, 0)

    def pair(jj, carry):
        j0 = 2 * jj
        scores(j0 + 1, 1)
        update(j0, 0)
        scores(j0 + 2, 0)
        update(j0 + 1, 1)
        return carry

    lax.fori_loop(0, qi, pair, 0)

    jd = 2 * qi
    scores(jd + 1, 1, lower)
    update(jd, 0, upper, masked=True)
    update(jd, 0, lower)
    update(jd + 1, 1, lower, masked=True)
    for hd in range(heads_per_step):
        acc = acc_scr[hd]
        denom = acc[:, MLA_V:MLA_V + 1]
        o_ref[:, hd * MLA_V:(hd + 1) * MLA_V] = (acc[:, :MLA_V] / denom).astype(BF16)


def _mla_attn(q, k, v, batch, seq, *, tk=512, heads_per_step=2):
    width = MLA_HEADS * MLA_HEAD_LANES
    tq = 2 * tk
    q3, k3, v3 = (a.reshape(batch, seq, width) for a in (q, k, v))
    hw = heads_per_step * MLA_HEAD_LANES
    out = pl.pallas_call(
        functools.partial(_mla_attn_kernel, tk=tk, heads_per_step=heads_per_step),
        grid=(batch, MLA_HEADS // heads_per_step, seq // tq),
        in_specs=[
            pl.BlockSpec((None, tq, hw), lambda b, hp, i: (b, i, hp)),
            pl.BlockSpec((None, seq, hw), lambda b, hp, i: (b, 0, hp)),
            pl.BlockSpec((None, seq, hw), lambda b, hp, i: (b, 0, hp)),
        ],
        out_specs=pl.BlockSpec((None, tq, heads_per_step * MLA_V), lambda b, hp, i: (b, i, hp)),
        out_shape=jax.ShapeDtypeStruct((batch, seq, MLA_WIDTH), BF16),
        scratch_shapes=[
            pltpu.VMEM((2, heads_per_step, tq, tk), F32),
            pltpu.VMEM((heads_per_step, tq, tk), BF16),
            pltpu.VMEM((heads_per_step, tq, 1), F32),
            pltpu.VMEM((heads_per_step, tq, 1), F32),
            pltpu.VMEM((heads_per_step, tq, MLA_HEAD_LANES), F32),
        ],
        compiler_params=_params(3),
        name="mla_attn",
    )(q3, k3, v3)
    return out.reshape(batch * seq, MLA_WIDTH)


def _dil_attn_kernel(q_ref, kp_ref, kc_ref, vp_ref, vc_ref, o_ref, lse_ref,
                     kk_scr, vv_scr, *, nb, nr):
    chunk = pl.program_id(2)
    blk = DIL_BLOCK
    kk_scr[:, :blk, :] = kp_ref[...]
    kk_scr[:, blk:, :] = kc_ref[...]
    for pair in range(DIL_HPG // 2):
        src = slice(pair * LANES, (pair + 1) * LANES)
        dst = slice(2 * pair * LANES, (2 * pair + 1) * LANES)
        one = slice((2 * pair + 1) * LANES, (2 * pair + 2) * LANES)
        vv_scr[:, :blk, dst] = vp_ref[:, :, src]
        vv_scr[:, blk:, dst] = vc_ref[:, :, src]
        vv_scr[:, :, one] = jnp.ones(vv_scr.shape[:2] + (LANES,), BF16)

    r = lax.broadcasted_iota(jnp.int32, (blk, 2 * blk), 0)
    cidx = lax.broadcasted_iota(jnp.int32, (blk, 2 * blk), 1)
    dist = r + blk - cidx
    band = jnp.where((dist >= 0) & (dist <= blk), 0.0, NEG_BIG)
    first_band = jnp.where((cidx >= blk) | (chunk > 0), band, NEG_BIG)
    lane = lax.broadcasted_iota(jnp.int32, (blk, LANES), 1)
    low = lane < DIL_DH

    for ri in range(nr):
        for j in range(nb):
            rows = slice(j * blk, (j + 1) * blk)
            band_rows = slice(j * blk, (j + 2) * blk)
            bias = first_band if j == 0 else band
            for pair in range(DIL_HPG // 2):
                cols = slice(pair * LANES, (pair + 1) * LANES)
                q2 = q_ref[ri, rows, cols]
                k2 = kk_scr[ri, band_rows, cols]
                v2 = vv_scr[ri, band_rows, 2 * pair * LANES:(2 * pair + 2) * LANES]
                qs = jnp.concatenate([jnp.where(low, q2, jnp.zeros_like(q2)),
                                      jnp.where(low, jnp.zeros_like(q2), q2)], axis=0)
                s = lax.dot_general(qs, k2, _NT, preferred_element_type=F32)
                s = s + jnp.concatenate([bias, bias], axis=0)
                m = jnp.max(s, axis=-1, keepdims=True)
                p = jnp.exp2(s - m)
                pv = jnp.dot(p.astype(BF16), v2, preferred_element_type=F32)
                num = jnp.where(low, pv[:blk, :LANES], pv[blk:, :LANES])
                denom = jnp.where(low, pv[:blk, LANES:], pv[blk:, LANES:])
                m_sel = jnp.where(low, m[:blk], m[blk:])
                o_ref[ri, rows, cols] = (num / denom).astype(BF16)
                lse_ref[ri, rows, cols] = m_sel + jnp.log2(denom)


def _dil_attn(qkv, group):
    batch, d, sub_len, _ = qkv.shape
    nb = DIL_BLOCKS_PER_STEP[group]
    nr = DIL_SUBSEQ_PER_STEP[group]
    blk = DIL_BLOCK
    rows = nb * blk
    cur = lambda part: pl.BlockSpec((None, nr, rows, DIL_WIDTH),
                                    lambda b, r, n: (b, r, n, part))
    prev = lambda part: pl.BlockSpec((None, nr, blk, DIL_WIDTH),
                                     lambda b, r, n: (b, r, jnp.maximum(n * nb - 1, 0), part))
    out = pl.BlockSpec((None, nr, rows, DIL_WIDTH), lambda b, r, n: (b, r, n, 0))
    return pl.pallas_call(
        functools.partial(_dil_attn_kernel, nb=nb, nr=nr),
        grid=(batch, d // nr, sub_len // rows),
        in_specs=[cur(0), prev(1), cur(1), prev(2), cur(2)],
        out_specs=[out, out],
        out_shape=[
            jax.ShapeDtypeStruct((batch, d, sub_len, DIL_WIDTH), BF16),
            jax.ShapeDtypeStruct((batch, d, sub_len, DIL_WIDTH), F32),
        ],
        scratch_shapes=[pltpu.VMEM((nr, rows + blk, DIL_WIDTH), BF16),
                        pltpu.VMEM((nr, rows + blk, 2 * DIL_WIDTH), BF16)],
        compiler_params=_params(3),
        name=f"dil_attn_{group}",
    )(qkv, qkv, qkv, qkv, qkv)


def _out_kernel(x_ref, ymla_ref, o0_ref, o1_ref, o2_ref, l0_ref, l1_ref, l2_ref,
                z_ref, gate_ref, wpm_ref, wpd_ref, wo_ref, pg_ref, out_ref, perm_scr, *, tm):
    n_sl = DIL_WIDTH // LANES

    def to_token_order(ref, d, slot):
        rows = tm // d
        for r in range(d):
            for s in range(n_sl):
                perm_scr[slot * n_sl + s, pl.ds(r, rows, stride=d), :] = (
                    ref[r, :, s * LANES:(s + 1) * LANES].astype(F32))

    def token_rows(slot, rs):
        return jnp.concatenate([perm_scr[slot * n_sl + s, rs, :] for s in range(n_sl)], axis=1)

    d1, d2 = DIL_DILATIONS[1], DIL_DILATIONS[2]
    to_token_order(o1_ref, d1, 0)
    to_token_order(l1_ref, d1, 1)
    to_token_order(o2_ref, d2, 2)
    to_token_order(l2_ref, d2, 3)

    for r0 in range(0, tm, tm // 2):
        rs = slice(r0, r0 + tm // 2)
        o0, l0 = o0_ref[0, rs, :].astype(F32), l0_ref[0, rs, :]
        o1, l1 = token_rows(0, rs), token_rows(1, rs)
        o2, l2 = token_rows(2, rs), token_rows(3, rs)
        m = jnp.maximum(jnp.maximum(l0, l1), l2)
        e0, e1, e2 = jnp.exp2(l0 - m), jnp.exp2(l1 - m), jnp.exp2(l2 - m)
        y_dil = (e0 * o0 + e1 * o1 + e2 * o2) / (e0 + e1 + e2)

        a = (ymla_ref[rs, :].astype(F32) * z_ref[rs, :MLA_WIDTH].astype(F32)).astype(BF16)
        b = (y_dil * z_ref[rs, MLA_WIDTH:].astype(F32)).astype(BF16)
        pa = jnp.dot(a, wpm_ref[...], preferred_element_type=F32)
        pb = jnp.dot(b, wpd_ref[...], preferred_element_type=F32)
        merged = (gate_ref[rs, :D_MODEL].astype(F32) * pa
                  + gate_ref[rs, D_MODEL:].astype(F32) * pb)
        u = jnp.dot(merged.astype(BF16), wo_ref[...], preferred_element_type=F32)
        out_ref[rs, :] = x_ref[rs, :] + _rms(u, pg_ref[...])


def _out_block(x2, y_mla, o_dil, lse_dil, z, gate, wpm, wpd, wo, pg, seq, *, tm=512):
    t = x2.shape[0]
    tiles_per_seq = seq // tm
    row = lambda i: (i, 0)
    const = lambda i: (0, 0)
    rows = lambda w: pl.BlockSpec((tm, w), row)
    sub = lambda d: pl.BlockSpec((None, d, tm // d, DIL_WIDTH),
                                 lambda i: (i // tiles_per_seq, 0, i % tiles_per_seq, 0))
    subs = [sub(d) for d in DIL_DILATIONS]
    return pl.pallas_call(
        functools.partial(_out_kernel, tm=tm),
        grid=(t // tm,),
        in_specs=[rows(D_MODEL), rows(MLA_WIDTH)] + subs + subs
        + [rows(Z_WIDTH), rows(G_WIDTH),
           pl.BlockSpec((MLA_WIDTH, D_MODEL), const),
           pl.BlockSpec((DIL_WIDTH, D_MODEL), const),
           pl.BlockSpec((D_MODEL, D_MODEL), const),
           pl.BlockSpec((1, D_MODEL), const)],
        out_specs=rows(D_MODEL),
        out_shape=jax.ShapeDtypeStruct((t, D_MODEL), F32),
        scratch_shapes=[pltpu.VMEM((4 * DIL_WIDTH // LANES, tm, LANES), F32)],
        compiler_params=_params(1),
        name="out_block",
    )(x2, y_mla, *o_dil, *lse_dil, z, gate, wpm, wpd, wo, pg)


def _inv_freq(rot_dim):
    return ROPE_THETA ** (-jnp.arange(0, rot_dim, 2, dtype=F32) / rot_dim)


def _lane_freqs():
    f = jnp.zeros((LANES,), F32)
    f = f.at[:DIL_ROPE].set(jnp.tile(_inv_freq(DIL_ROPE), 2))
    f = f.at[MLA_NOPE:MLA_NOPE + MLA_ROPE].set(jnp.tile(_inv_freq(MLA_ROPE), 2))
    return f[None, :]


def _w_cat_kernel(w_ref, o_ref):
    lat = MLA_Q_RANK + MLA_KV_RANK
    cols = w_ref.shape[1]
    o_ref[:lat, :] = w_ref[:lat, :].astype(BF16)
    o_ref[lat:lat + MLA_NOPE, :] = jnp.zeros((MLA_NOPE, cols), BF16)
    o_ref[lat + MLA_NOPE:lat + MLA_NOPE + MLA_ROPE, :] = w_ref[lat:lat + MLA_ROPE, :].astype(BF16)
    o_ref[lat + MLA_NOPE + MLA_ROPE:LAT_WIDTH, :] = jnp.zeros(
        (LANES - MLA_NOPE - MLA_ROPE, cols), BF16)
    o_ref[LAT_WIDTH:, :] = w_ref[lat + MLA_ROPE:, :].astype(BF16)


def _w_cat(w_in_t, layer, *, cols=128):
    _, n, k = w_in_t.shape
    return pl.pallas_call(
        _w_cat_kernel,
        grid=(k // cols,),
        in_specs=[pl.BlockSpec((None, n, cols), lambda i: (layer, 0, i))],
        out_specs=pl.BlockSpec((IN_TOTAL, cols), lambda i: (0, i)),
        out_shape=jax.ShapeDtypeStruct((IN_TOTAL, k), BF16),
        compiler_params=_params(1),
        name="w_cat",
    )(w_in_t)


def _layer_weights(w_in, layer, w_uq, w_ukv):
    w_cat = _w_cat(jnp.swapaxes(w_in, 1, 2), layer)

    wq = w_uq.reshape(MLA_Q_RANK, MLA_HEADS, MLA_NOPE + MLA_ROPE)
    wq = jnp.pad(wq, ((0, 0), (0, 0), (0, MLA_HEAD_LANES - MLA_NOPE - MLA_ROPE)))
    wq = wq.reshape(MLA_Q_RANK, MLA_HEADS * MLA_HEAD_LANES).astype(BF16)
    return w_cat, wq, w_ukv.astype(BF16)


def kernel(x, positions, pre_norm_g, w_in, q_norm_g, w_uq, kv_norm_g, w_ukv,
           w_proj_mla, w_proj_dil, w_out, post_norm_g):
    batch, seq, _ = x.shape
    depth = w_in.shape[0]
    pos_col = positions.reshape(batch * seq, 1)
    freqs = _lane_freqs()
    x2 = x.reshape(batch * seq, D_MODEL)
    for layer in range(depth):
        w_cat, wq, wkv = _layer_weights(w_in, layer, w_uq[layer], w_ukv[layer])
        q, k, v, qkv0, qkv1, qkv2, z, gate = _in_proj(
            x2, pos_col, pre_norm_g[layer][None, :], w_cat, freqs,
            q_norm_g[layer][None, :], kv_norm_g[layer][None, :], wq, wkv, batch, seq)
        y_mla = _mla_attn(q, k, v, batch, seq)
        dil = [_dil_attn(qkv, g) for g, qkv in enumerate((qkv0, qkv1, qkv2))]
        x2 = _out_block(x2, y_mla, [o for o, _ in dil], [l for _, l in dil], z, gate,
                        w_proj_mla[layer].astype(BF16), w_proj_dil[layer].astype(BF16),
                        w_out[layer].astype(BF16), post_norm_g[layer][None, :], seq)
    return x2.reshape(batch, seq, D_MODEL)
```

```python
import functools
import math

import jax
import jax.numpy as jnp
from jax import lax
from jax.experimental import pallas as pl
from jax.experimental.pallas import tpu as pltpu

F32 = jnp.float32
BF16 = jnp.bfloat16

D_MODEL = 1024
ROPE_THETA = 500000.0
NORM_EPS = 1e-6

MLA_HEADS = 8
MLA_Q_RANK = 384
MLA_KV_RANK = 256
MLA_NOPE = 64
MLA_ROPE = 32
MLA_V = 64
MLA_WIDTH = MLA_HEADS * MLA_V
MLA_HEAD_LANES = 128
MLA_ROW_CHUNK = 64

DIL_DILATIONS = (1, 4, 16)
DIL_GROUPS = 3
DIL_HPG = 8
DIL_DH = 64
DIL_ROPE = 16
DIL_WIDTH = DIL_HPG * DIL_DH
DIL_BLOCK = 128
DIL_BLOCKS_PER_STEP = (8, 8, 2)
DIL_SUBSEQ_PER_STEP = (1, 1, 4)

LANES = 128
NEG_BIG = -1e30
LOG2E = math.log2(math.e)

VMEM_LIMIT = 56 * 1024 * 1024

LAT_WIDTH = MLA_Q_RANK + MLA_KV_RANK + LANES
QKV_WIDTH = 3 * DIL_WIDTH
Z_WIDTH = MLA_WIDTH + DIL_WIDTH
G_WIDTH = 2 * D_MODEL
IN_TOTAL = LAT_WIDTH + DIL_GROUPS * QKV_WIDTH + Z_WIDTH + G_WIDTH
N_SLABS = D_MODEL // LANES


def _params(n_axes):
    return pltpu.CompilerParams(
        dimension_semantics=("arbitrary",) * n_axes, vmem_limit_bytes=VMEM_LIMIT)


def _trig_tables(pos_col, freq_row):
    ang = pos_col.astype(F32) * freq_row
    return jnp.cos(ang), jnp.sin(ang)


def _rope_tables(cos, sin, lane_lo, width, period):
    lane = lax.broadcasted_iota(jnp.int32, cos.shape, 1) % period
    in_a = (lane >= lane_lo) & (lane < lane_lo + width)
    in_b = (lane >= lane_lo + width) & (lane < lane_lo + 2 * width)
    c = jnp.where(in_a | in_b, cos, 1.0)
    return c, jnp.where(in_a, -sin, 0.0), jnp.where(in_b, sin, 0.0)


def _apply_rope(a, c, sa, sb, shift):
    return a * c + pltpu.roll(a, LANES - shift, 1) * sa + pltpu.roll(a, shift, 1) * sb


def _rms(x, g):
    ms = jnp.mean(x * x, axis=-1, keepdims=True)
    return x * lax.rsqrt(ms + NORM_EPS) * g


def _sigmoid(x):
    return 1.0 / (1.0 + jnp.exp(-x))


def _in_proj_kernel(x_ref, pos_ref, g_ref, w_ref, freq_ref, qg_ref, kvg_ref,
                    wq_ref, wkv_ref,
                    q_ref, k_ref, v_ref, qkv0_ref, qkv1_ref, qkv2_ref, z_ref, gate_ref,
                    h_scr, tab_scr, hp_scr, *, tm):
    h = _rms(x_ref[...], g_ref[...])
    hb = h.astype(BF16)
    cos, sin = _trig_tables(pos_ref[...], freq_ref[...])

    def proj(lhs, c0, width):
        return lax.dot_general(lhs, w_ref[c0:c0 + width, :], _NT, preferred_element_type=F32)

    lat = proj(hb, 0, LAT_WIDTH)
    cq = _rms(lat[:, :MLA_Q_RANK], qg_ref[...]).astype(BF16)
    ckv = _rms(lat[:, MLA_Q_RANK:MLA_Q_RANK + MLA_KV_RANK], kvg_ref[...]).astype(BF16)
    mhalf = MLA_ROPE // 2
    mtabs = _rope_tables(cos, sin, MLA_NOPE, mhalf, LANES)
    kr = _apply_rope(lat[:, MLA_Q_RANK + MLA_KV_RANK:], *mtabs, mhalf)
    q_scale = (MLA_NOPE + MLA_ROPE) ** -0.5 * LOG2E
    lane = lax.broadcasted_iota(jnp.int32, kr.shape, 1)
    q_all = jnp.dot(cq, wq_ref[...], preferred_element_type=F32)
    kv_all = jnp.dot(ckv, wkv_ref[...], preferred_element_type=F32)
    for hd in range(MLA_HEADS):
        sl = slice(hd * MLA_HEAD_LANES, (hd + 1) * MLA_HEAD_LANES)
        q_ref[:, sl] = (_apply_rope(q_all[:, sl], *mtabs, mhalf) * q_scale).astype(BF16)
        kv = kv_all[:, sl]
        k_ref[:, sl] = jnp.where(lane < MLA_NOPE, kv, kr).astype(BF16)
        v_ref[:, sl] = jnp.where(lane == MLA_V, 1.0, pltpu.roll(kv, MLA_NOPE, 1)).astype(BF16)

    dhalf = DIL_ROPE // 2
    first = lax.broadcasted_iota(jnp.int32, cos.shape, 1) < DIL_DH
    dtabs = _rope_tables(jnp.where(first, cos, pltpu.roll(cos, DIL_DH, 1)),
                         jnp.where(first, sin, pltpu.roll(sin, DIL_DH, 1)), 0, dhalf, DIL_DH)
    d_scale = DIL_DH ** -0.5 * LOG2E

    def qkv_pieces(lhs, group, tables):
        for part in range(3):
            acc = proj(lhs, LAT_WIDTH + (part * DIL_GROUPS + group) * DIL_WIDTH, DIL_WIDTH)
            for l0 in range(0, DIL_WIDTH, LANES):
                a = acc[:, l0:l0 + LANES]
                if part < 2:
                    a = _apply_rope(a, *tables, dhalf)
                if part == 0:
                    a = a * d_scale
                yield part * DIL_WIDTH + l0, a.astype(BF16)

    for off, piece in qkv_pieces(hb, 0, dtabs):
        qkv0_ref[0, :, off:off + LANES] = piece

    for s in range(N_SLABS):
        h_scr[s] = h[:, s * LANES:(s + 1) * LANES]
    for i, t in enumerate(dtabs):
        tab_scr[i] = t
    for group, out_ref in ((1, qkv1_ref), (2, qkv2_ref)):
        d = DIL_DILATIONS[group]
        rows = tm // d
        for r in range(d):
            for s in range(N_SLABS):
                hp_scr[r * rows:(r + 1) * rows, s * LANES:(s + 1) * LANES] = (
                    h_scr[s, pl.ds(r, rows, stride=d), :].astype(BF16))
        ptabs = tuple(
            jnp.concatenate([tab_scr[i, pl.ds(r, rows, stride=d), :] for r in range(d)], axis=0)
            for i in range(3))
        for off, piece in qkv_pieces(hp_scr[...], group, ptabs):
            for r in range(d):
                out_ref[r, :, off:off + LANES] = piece[r * rows:(r + 1) * rows, :]

    col = LAT_WIDTH + DIL_GROUPS * QKV_WIDTH
    for c0 in range(0, Z_WIDTH, 512):
        zacc = proj(hb, col + c0, 512)
        z_ref[:, c0:c0 + 512] = (zacc * _sigmoid(zacc)).astype(BF16)
    col += Z_WIDTH
    for c0 in range(0, G_WIDTH, 512):
        gate_ref[:, c0:c0 + 512] = _sigmoid(proj(hb, col + c0, 512)).astype(BF16)


def _in_proj(x2, pos_col, g_pre, w_cat, freqs, qg, kvg, wq, wkv, batch, seq, *, tm=512):
    t = x2.shape[0]
    tiles_per_seq = seq // tm
    mla_width = MLA_HEADS * MLA_HEAD_LANES
    row = lambda i: (i, 0)
    const = lambda i: (0, 0)
    resident = lambda shape: pl.BlockSpec(shape, const, pipeline_mode=pl.Buffered(1))

    def qkv_spec(d):
        return pl.BlockSpec((None, d, tm // d, QKV_WIDTH),
                            lambda i: (i // tiles_per_seq, 0, i % tiles_per_seq, 0))

    return pl.pallas_call(
        functools.partial(_in_proj_kernel, tm=tm),
        grid=(t // tm,),
        in_specs=[
            pl.BlockSpec((tm, D_MODEL), row),
            pl.BlockSpec((tm, 1), row),
            resident((1, D_MODEL)),
            resident((IN_TOTAL, D_MODEL)),
            resident((1, LANES)),
            resident((1, MLA_Q_RANK)),
            resident((1, MLA_KV_RANK)),
            resident((MLA_Q_RANK, mla_width)),
            resident((MLA_KV_RANK, mla_width)),
        ],
        out_specs=[pl.BlockSpec((tm, mla_width), row)] * 3
        + [qkv_spec(d) for d in DIL_DILATIONS]
        + [pl.BlockSpec((tm, Z_WIDTH), row), pl.BlockSpec((tm, G_WIDTH), row)],
        out_shape=[jax.ShapeDtypeStruct((t, mla_width), BF16)] * 3
        + [jax.ShapeDtypeStruct((batch, d, seq // d, QKV_WIDTH), BF16) for d in DIL_DILATIONS]
        + [jax.ShapeDtypeStruct((t, Z_WIDTH), BF16), jax.ShapeDtypeStruct((t, G_WIDTH), BF16)],
        scratch_shapes=[
            pltpu.VMEM((N_SLABS, tm, LANES), F32),
            pltpu.VMEM((3, tm, LANES), F32),
            pltpu.VMEM((tm, D_MODEL), BF16),
        ],
        compiler_params=_params(1),
        name="in_proj",
    )(x2, pos_col, g_pre, w_cat, freqs, qg, kvg, wq, wkv)


_NT = (((1,), (1,)), ((), ()))


def _mla_attn_kernel(q_ref, k_ref, v_ref, o_ref, s_scr, p_scr, m_scr, alpha_scr, acc_scr, *,
                     tk, heads_per_step):
    tq = 2 * tk
    seq = q_ref.shape[0]
    slices = [slice(hd * MLA_HEAD_LANES, (hd + 1) * MLA_HEAD_LANES)
              for hd in range(heads_per_step)]
    every = slice(0, tq)
    upper, lower = slice(0, tk), slice(tk, tq)

    for qi in range(seq // tq):
        q0 = qi * tq

        def scores(j, slot, rows=every):
            for hd, sl in enumerate(slices):
                s_scr[slot, hd, rows, :] = lax.dot_general(
                    q_ref[q0 + rows.start:q0 + rows.stop, sl], k_ref[j * tk:(j + 1) * tk, sl],
                    _NT, preferred_element_type=F32)

        def update(j, slot, rows=every, masked=False):
            for hd, sl in enumerate(slices):
                for c0 in range(rows.start, rows.stop, MLA_ROW_CHUNK):
                    rc = slice(c0, c0 + MLA_ROW_CHUNK)
                    s = s_scr[slot, hd, rc, :]
                    if masked:
                        r = lax.broadcasted_iota(jnp.int32, s.shape, 0) + (c0 - rows.start)
                        cidx = lax.broadcasted_iota(jnp.int32, s.shape, 1)
                        s = jnp.where(cidx <= r, s, NEG_BIG)
                    m = m_scr[hd, rc, :]
                    m_new = jnp.maximum(m, jnp.max(s, axis=-1, keepdims=True))
                    p_scr[hd, rc, :] = jnp.exp2(s - m_new).astype(BF16)
                    alpha_scr[hd, rc, :] = jnp.exp2(m - m_new)
                    m_scr[hd, rc, :] = m_new
                acc_scr[hd, rows, :] = alpha_scr[hd, rows, :] * acc_scr[hd, rows, :] + jnp.dot(
                    p_scr[hd, rows, :], v_ref[j * tk:(j + 1) * tk, sl],
                    preferred_element_type=F32)

        m_scr[...] = jnp.full(m_scr.shape, NEG_BIG, F32)
        acc_scr[...] = jnp.zeros(acc_scr.shape, F32)
        scores(0, 0)
        for jj in range(qi):
            j0 = 2 * jj
            scores(j0 + 1, 1)
            update(j0, 0)
            scores(j0 + 2, 0)
            update(j0 + 1, 1)

        jd = 2 * qi
        scores(jd + 1, 1, lower)
        update(jd, 0, upper, masked=True)
        update(jd, 0, lower)
        update(jd + 1, 1, lower, masked=True)
        for hd in range(heads_per_step):
            acc = acc_scr[hd]
            denom = acc[:, MLA_V:MLA_V + 1]
            o_ref[q0:q0 + tq, hd * MLA_V:(hd + 1) * MLA_V] = (acc[:, :MLA_V] / denom).astype(BF16)


def _mla_attn(q, k, v, batch, seq, *, tk=512, heads_per_step=2):
    width = MLA_HEADS * MLA_HEAD_LANES
    tq = 2 * tk
    q3, k3, v3 = (a.reshape(batch, seq, width) for a in (q, k, v))
    hw = heads_per_step * MLA_HEAD_LANES
    whole = pl.BlockSpec((None, seq, hw), lambda b, hp: (b, 0, hp))
    out = pl.pallas_call(
        functools.partial(_mla_attn_kernel, tk=tk, heads_per_step=heads_per_step),
        grid=(batch, MLA_HEADS // heads_per_step),
        in_specs=[whole, whole, whole],
        out_specs=pl.BlockSpec((None, seq, heads_per_step * MLA_V), lambda b, hp: (b, 0, hp)),
        out_shape=jax.ShapeDtypeStruct((batch, seq, MLA_WIDTH), BF16),
        scratch_shapes=[
            pltpu.VMEM((2, heads_per_step, tq, tk), F32),
            pltpu.VMEM((heads_per_step, tq, tk), BF16),
            pltpu.VMEM((heads_per_step, tq, 1), F32),
            pltpu.VMEM((heads_per_step, tq, 1), F32),
            pltpu.VMEM((heads_per_step, tq, MLA_HEAD_LANES), F32),
        ],
        compiler_params=_params(2),
        name="mla_attn",
    )(q3, k3, v3)
    return out.reshape(batch * seq, MLA_WIDTH)


def _dil_attn_kernel(q_ref, kp_ref, kc_ref, vp_ref, vc_ref, o_ref, lse_ref,
                     kk_scr, vv_scr, *, nb, nr):
    chunk = pl.program_id(2)
    blk = DIL_BLOCK
    kk_scr[:, :blk, :] = kp_ref[...]
    kk_scr[:, blk:, :] = kc_ref[...]
    for pair in range(DIL_HPG // 2):
        src = slice(pair * LANES, (pair + 1) * LANES)
        dst = slice(2 * pair * LANES, (2 * pair + 1) * LANES)
        one = slice((2 * pair + 1) * LANES, (2 * pair + 2) * LANES)
        vv_scr[:, :blk, dst] = vp_ref[:, :, src]
        vv_scr[:, blk:, dst] = vc_ref[:, :, src]
        vv_scr[:, :, one] = jnp.ones(vv_scr.shape[:2] + (LANES,), BF16)

    r = lax.broadcasted_iota(jnp.int32, (blk, 2 * blk), 0)
    cidx = lax.broadcasted_iota(jnp.int32, (blk, 2 * blk), 1)
    dist = r + blk - cidx
    band = jnp.where((dist >= 0) & (dist <= blk), 0.0, NEG_BIG)
    first_band = jnp.where((cidx >= blk) | (chunk > 0), band, NEG_BIG)
    lane = lax.broadcasted_iota(jnp.int32, (blk, LANES), 1)
    low = lane < DIL_DH

    for ri in range(nr):
        for j in range(nb):
            rows = slice(j * blk, (j + 1) * blk)
            band_rows = slice(j * blk, (j + 2) * blk)
            bias = first_band if j == 0 else band
            for pair in range(DIL_HPG // 2):
                cols = slice(pair * LANES, (pair + 1) * LANES)
                q2 = q_ref[ri, rows, cols]
                k2 = kk_scr[ri, band_rows, cols]
                v2 = vv_scr[ri, band_rows, 2 * pair * LANES:(2 * pair + 2) * LANES]
                qs = jnp.concatenate([jnp.where(low, q2, jnp.zeros_like(q2)),
                                      jnp.where(low, jnp.zeros_like(q2), q2)], axis=0)
                s = lax.dot_general(qs, k2, _NT, preferred_element_type=F32)
                s = s + jnp.concatenate([bias, bias], axis=0)
                m = jnp.max(s, axis=-1, keepdims=True)
                p = jnp.exp2(s - m)
                pv = jnp.dot(p.astype(BF16), v2, preferred_element_type=F32)
                num = jnp.where(low, pv[:blk, :LANES], pv[blk:, :LANES])
                denom = jnp.where(low, pv[:blk, LANES:], pv[blk:, LANES:])
                m_sel = jnp.where(low, m[:blk], m[blk:])
                o_ref[ri, rows, cols] = (num / denom).astype(BF16)
                lse_ref[ri, rows, cols] = m_sel + jnp.log2(denom)


def _dil_attn(qkv, group):
    batch, d, sub_len, _ = qkv.shape
    nb = DIL_BLOCKS_PER_STEP[group]
    nr = DIL_SUBSEQ_PER_STEP[group]
    blk = DIL_BLOCK
    rows = nb * blk
    cur = lambda part: pl.BlockSpec((None, nr, rows, DIL_WIDTH),
                                    lambda b, r, n: (b, r, n, part))
    prev = lambda part: pl.BlockSpec((None, nr, blk, DIL_WIDTH),
                                     lambda b, r, n: (b, r, jnp.maximum(n * nb - 1, 0), part))
    out = pl.BlockSpec((None, nr, rows, DIL_WIDTH), lambda b, r, n: (b, r, n, 0))
    return pl.pallas_call(
        functools.partial(_dil_attn_kernel, nb=nb, nr=nr),
        grid=(batch, d // nr, sub_len // rows),
        in_specs=[cur(0), prev(1), cur(1), prev(2), cur(2)],
        out_specs=[out, out],
        out_shape=[
            jax.ShapeDtypeStruct((batch, d, sub_len, DIL_WIDTH), BF16),
            jax.ShapeDtypeStruct((batch, d, sub_len, DIL_WIDTH), F32),
        ],
        scratch_shapes=[pltpu.VMEM((nr, rows + blk, DIL_WIDTH), BF16),
                        pltpu.VMEM((nr, rows + blk, 2 * DIL_WIDTH), BF16)],
        compiler_params=_params(3),
        name=f"dil_attn_{group}",
    )(qkv, qkv, qkv, qkv, qkv)


def _out_kernel(x_ref, ymla_ref, o0_ref, o1_ref, o2_ref, l0_ref, l1_ref, l2_ref,
                z_ref, gate_ref, wpm_ref, wpd_ref, wo_ref, pg_ref, out_ref, perm_scr, *, tm):
    n_sl = DIL_WIDTH // LANES

    def to_token_order(ref, d, slot):
        rows = tm // d
        for r in range(d):
            for s in range(n_sl):
                perm_scr[slot * n_sl + s, pl.ds(r, rows, stride=d), :] = (
                    ref[r, :, s * LANES:(s + 1) * LANES].astype(F32))

    def token_rows(slot, rs):
        return jnp.concatenate([perm_scr[slot * n_sl + s, rs, :] for s in range(n_sl)], axis=1)

    d1, d2 = DIL_DILATIONS[1], DIL_DILATIONS[2]
    to_token_order(o1_ref, d1, 0)
    to_token_order(l1_ref, d1, 1)
    to_token_order(o2_ref, d2, 2)
    to_token_order(l2_ref, d2, 3)

    for r0 in range(0, tm, tm // 2):
        rs = slice(r0, r0 + tm // 2)
        o0, l0 = o0_ref[0, rs, :].astype(F32), l0_ref[0, rs, :]
        o1, l1 = token_rows(0, rs), token_rows(1, rs)
        o2, l2 = token_rows(2, rs), token_rows(3, rs)
        m = jnp.maximum(jnp.maximum(l0, l1), l2)
        e0, e1, e2 = jnp.exp2(l0 - m), jnp.exp2(l1 - m), jnp.exp2(l2 - m)
        y_dil = (e0 * o0 + e1 * o1 + e2 * o2) / (e0 + e1 + e2)

        a = (ymla_ref[rs, :].astype(F32) * z_ref[rs, :MLA_WIDTH].astype(F32)).astype(BF16)
        b = (y_dil * z_ref[rs, MLA_WIDTH:].astype(F32)).astype(BF16)
        pa = jnp.dot(a, wpm_ref[...], preferred_element_type=F32)
        pb = jnp.dot(b, wpd_ref[...], preferred_element_type=F32)
        merged = (gate_ref[rs, :D_MODEL].astype(F32) * pa
                  + gate_ref[rs, D_MODEL:].astype(F32) * pb)
        u = jnp.dot(merged.astype(BF16), wo_ref[...], preferred_element_type=F32)
        out_ref[rs, :] = x_ref[rs, :] + _rms(u, pg_ref[...])


def _out_block(x2, y_mla, o_dil, lse_dil, z, gate, wpm, wpd, wo, pg, seq, *, tm=512):
    t = x2.shape[0]
    tiles_per_seq = seq // tm
    row = lambda i: (i, 0)
    const = lambda i: (0, 0)
    rows = lambda w: pl.BlockSpec((tm, w), row)
    sub = lambda d: pl.BlockSpec((None, d, tm // d, DIL_WIDTH),
                                 lambda i: (i // tiles_per_seq, 0, i % tiles_per_seq, 0))
    subs = [sub(d) for d in DIL_DILATIONS]
    return pl.pallas_call(
        functools.partial(_out_kernel, tm=tm),
        grid=(t // tm,),
        in_specs=[rows(D_MODEL), rows(MLA_WIDTH)] + subs + subs
        + [rows(Z_WIDTH), rows(G_WIDTH),
           pl.BlockSpec((MLA_WIDTH, D_MODEL), const),
           pl.BlockSpec((DIL_WIDTH, D_MODEL), const),
           pl.BlockSpec((D_MODEL, D_MODEL), const),
           pl.BlockSpec((1, D_MODEL), const)],
        out_specs=rows(D_MODEL),
        out_shape=jax.ShapeDtypeStruct((t, D_MODEL), F32),
        scratch_shapes=[pltpu.VMEM((4 * DIL_WIDTH // LANES, tm, LANES), F32)],
        compiler_params=_params(1),
        name="out_block",
    )(x2, y_mla, *o_dil, *lse_dil, z, gate, wpm, wpd, wo, pg)


def _inv_freq(rot_dim):
    return ROPE_THETA ** (-jnp.arange(0, rot_dim, 2, dtype=F32) / rot_dim)


def _lane_freqs():
    f = jnp.zeros((LANES,), F32)
    f = f.at[:DIL_ROPE].set(jnp.tile(_inv_freq(DIL_ROPE), 2))
    f = f.at[MLA_NOPE:MLA_NOPE + MLA_ROPE].set(jnp.tile(_inv_freq(MLA_ROPE), 2))
    return f[None, :]


def _w_cat_kernel(w_ref, o_ref):
    lat = MLA_Q_RANK + MLA_KV_RANK
    cols = w_ref.shape[1]
    o_ref[:lat, :] = w_ref[:lat, :].astype(BF16)
    o_ref[lat:lat + MLA_NOPE, :] = jnp.zeros((MLA_NOPE, cols), BF16)
    o_ref[lat + MLA_NOPE:lat + MLA_NOPE + MLA_ROPE, :] = w_ref[lat:lat + MLA_ROPE, :].astype(BF16)
    o_ref[lat + MLA_NOPE + MLA_ROPE:LAT_WIDTH, :] = jnp.zeros(
        (LANES - MLA_NOPE - MLA_ROPE, cols), BF16)
    o_ref[LAT_WIDTH:, :] = w_ref[lat + MLA_ROPE:, :].astype(BF16)


def _w_cat(w_in_t, layer, *, cols=128):
    _, n, k = w_in_t.shape
    return pl.pallas_call(
        _w_cat_kernel,
        grid=(k // cols,),
        in_specs=[pl.BlockSpec((None, n, cols), lambda i: (layer, 0, i))],
        out_specs=pl.BlockSpec((IN_TOTAL, cols), lambda i: (0, i)),
        out_shape=jax.ShapeDtypeStruct((IN_TOTAL, k), BF16),
        compiler_params=_params(1),
        name="w_cat",
    )(w_in_t)


def _layer_weights(w_in, layer, w_uq, w_ukv):
    w_cat = _w_cat(jnp.swapaxes(w_in, 1, 2), layer)

    wq = w_uq.reshape(MLA_Q_RANK, MLA_HEADS, MLA_NOPE + MLA_ROPE)
    wq = jnp.pad(wq, ((0, 0), (0, 0), (0, MLA_HEAD_LANES - MLA_NOPE - MLA_ROPE)))
    wq = wq.reshape(MLA_Q_RANK, MLA_HEADS * MLA_HEAD_LANES).astype(BF16)
    return w_cat, wq, w_ukv.astype(BF16)


def kernel(x, positions, pre_norm_g, w_in, q_norm_g, w_uq, kv_norm_g, w_ukv,
           w_proj_mla, w_proj_dil, w_out, post_norm_g):
    batch, seq, _ = x.shape
    depth = w_in.shape[0]
    pos_col = positions.reshape(batch * seq, 1)
    freqs = _lane_freqs()
    x2 = x.reshape(batch * seq, D_MODEL)
    for layer in range(depth):
        w_cat, wq, wkv = _layer_weights(w_in, layer, w_uq[layer], w_ukv[layer])
        q, k, v, qkv0, qkv1, qkv2, z, gate = _in_proj(
            x2, pos_col, pre_norm_g[layer][None, :], w_cat, freqs,
            q_norm_g[layer][None, :], kv_norm_g[layer][None, :], wq, wkv, batch, seq)
        y_mla = _mla_attn(q, k, v, batch, seq)
        dil = [_dil_attn(qkv, g) for g, qkv in enumerate((qkv0, qkv1, qkv2))]
        x2 = _out_block(x2, y_mla, [o for o, _ in dil], [l for _, l in dil], z, gate,
                        w_proj_mla[layer].astype(BF16), w_proj_dil[layer].astype(BF16),
                        w_out[layer].astype(BF16), post_norm_g[layer][None, :], seq)
    return x2.reshape(batch, seq, D_MODEL)
```

```python
import functools
import math

import jax
import jax.numpy as jnp
from jax import lax
from jax.experimental import pallas as pl
from jax.experimental.pallas import tpu as pltpu

F32 = jnp.float32
BF16 = jnp.bfloat16

D_MODEL = 1024
ROPE_THETA = 500000.0
NORM_EPS = 1e-6

MLA_HEADS = 8
MLA_Q_RANK = 384
MLA_KV_RANK = 256
MLA_NOPE = 64
MLA_ROPE = 32
MLA_V = 64
MLA_WIDTH = MLA_HEADS * MLA_V
MLA_HEAD_LANES = 128
MLA_ROW_CHUNK = 64

DIL_DILATIONS = (1, 4, 16)
DIL_GROUPS = 3
DIL_HPG = 8
DIL_DH = 64
DIL_ROPE = 16
DIL_WIDTH = DIL_HPG * DIL_DH
DIL_BLOCK = 128
DIL_BLOCKS_PER_STEP = (8, 8, 2)
DIL_SUBSEQ_PER_STEP = (1, 1, 4)

LANES = 128
NEG_BIG = -1e30
LOG2E = math.log2(math.e)

VMEM_LIMIT = 56 * 1024 * 1024

LAT_WIDTH = MLA_Q_RANK + MLA_KV_RANK + LANES
QKV_WIDTH = 3 * DIL_WIDTH
Z_WIDTH = MLA_WIDTH + DIL_WIDTH
G_WIDTH = 2 * D_MODEL
IN_TOTAL = LAT_WIDTH + DIL_GROUPS * QKV_WIDTH + Z_WIDTH + G_WIDTH
N_SLABS = D_MODEL // LANES


def _params(n_axes):
    return pltpu.CompilerParams(
        dimension_semantics=("arbitrary",) * n_axes, vmem_limit_bytes=VMEM_LIMIT)


def _trig_tables(pos_col, freq_row):
    ang = pos_col.astype(F32) * freq_row
    return jnp.cos(ang), jnp.sin(ang)


def _rope_tables(cos, sin, lane_lo, width, period):
    lane = lax.broadcasted_iota(jnp.int32, cos.shape, 1) % period
    in_a = (lane >= lane_lo) & (lane < lane_lo + width)
    in_b = (lane >= lane_lo + width) & (lane < lane_lo + 2 * width)
    c = jnp.where(in_a | in_b, cos, 1.0)
    return c, jnp.where(in_a, -sin, 0.0), jnp.where(in_b, sin, 0.0)


def _apply_rope(a, c, sa, sb, shift):
    return a * c + pltpu.roll(a, LANES - shift, 1) * sa + pltpu.roll(a, shift, 1) * sb


def _rms(x, g):
    ms = jnp.mean(x * x, axis=-1, keepdims=True)
    return x * lax.rsqrt(ms + NORM_EPS) * g


def _sigmoid(x):
    return 1.0 / (1.0 + jnp.exp(-x))


def _in_proj_kernel(x_ref, pos_ref, g_ref, w_ref, freq_ref, qg_ref, kvg_ref,
                    wq_ref, wkv_ref,
                    q_ref, k_ref, v_ref, qkv0_ref, qkv1_ref, qkv2_ref, z_ref, gate_ref,
                    h_scr, tab_scr, hp_scr, *, tm):
    h = _rms(x_ref[...], g_ref[...])
    hb = h.astype(BF16)
    cos, sin = _trig_tables(pos_ref[...], freq_ref[...])

    def proj(lhs, c0, width):
        return lax.dot_general(lhs, w_ref[c0:c0 + width, :], _NT, preferred_element_type=F32)

    lat = proj(hb, 0, LAT_WIDTH)
    cq = _rms(lat[:, :MLA_Q_RANK], qg_ref[...]).astype(BF16)
    ckv = _rms(lat[:, MLA_Q_RANK:MLA_Q_RANK + MLA_KV_RANK], kvg_ref[...]).astype(BF16)
    mhalf = MLA_ROPE // 2
    mtabs = _rope_tables(cos, sin, MLA_NOPE, mhalf, LANES)
    kr = _apply_rope(lat[:, MLA_Q_RANK + MLA_KV_RANK:], *mtabs, mhalf)
    q_scale = (MLA_NOPE + MLA_ROPE) ** -0.5 * LOG2E
    lane = lax.broadcasted_iota(jnp.int32, kr.shape, 1)
    q_all = jnp.dot(cq, wq_ref[...], preferred_element_type=F32)
    kv_all = jnp.dot(ckv, wkv_ref[...], preferred_element_type=F32)
    for hd in range(MLA_HEADS):
        sl = slice(hd * MLA_HEAD_LANES, (hd + 1) * MLA_HEAD_LANES)
        q_ref[:, sl] = (_apply_rope(q_all[:, sl], *mtabs, mhalf) * q_scale).astype(BF16)
        kv = kv_all[:, sl]
        k_ref[:, sl] = jnp.where(lane < MLA_NOPE, kv, kr).astype(BF16)
        v_ref[:, sl] = jnp.where(lane == MLA_V, 1.0, pltpu.roll(kv, MLA_NOPE, 1)).astype(BF16)

    dhalf = DIL_ROPE // 2
    first = lax.broadcasted_iota(jnp.int32, cos.shape, 1) < DIL_DH
    dtabs = _rope_tables(jnp.where(first, cos, pltpu.roll(cos, DIL_DH, 1)),
                         jnp.where(first, sin, pltpu.roll(sin, DIL_DH, 1)), 0, dhalf, DIL_DH)
    d_scale = DIL_DH ** -0.5 * LOG2E

    def qkv_pieces(lhs, group, tables):
        for part in range(3):
            acc = proj(lhs, LAT_WIDTH + (part * DIL_GROUPS + group) * DIL_WIDTH, DIL_WIDTH)
            for l0 in range(0, DIL_WIDTH, LANES):
                a = acc[:, l0:l0 + LANES]
                if part < 2:
                    a = _apply_rope(a, *tables, dhalf)
                if part == 0:
                    a = a * d_scale
                yield part * DIL_WIDTH + l0, a.astype(BF16)

    for off, piece in qkv_pieces(hb, 0, dtabs):
        qkv0_ref[0, :, off:off + LANES] = piece

    for s in range(N_SLABS):
        h_scr[s] = h[:, s * LANES:(s + 1) * LANES]
    for i, t in enumerate(dtabs):
        tab_scr[i] = t
    for group, out_ref in ((1, qkv1_ref), (2, qkv2_ref)):
        d = DIL_DILATIONS[group]
        rows = tm // d
        for r in range(d):
            for s in range(N_SLABS):
                hp_scr[r * rows:(r + 1) * rows, s * LANES:(s + 1) * LANES] = (
                    h_scr[s, pl.ds(r, rows, stride=d), :].astype(BF16))
        ptabs = tuple(
            jnp.concatenate([tab_scr[i, pl.ds(r, rows, stride=d), :] for r in range(d)], axis=0)
            for i in range(3))
        for off, piece in qkv_pieces(hp_scr[...], group, ptabs):
            for r in range(d):
                out_ref[r, :, off:off + LANES] = piece[r * rows:(r + 1) * rows, :]

    col = LAT_WIDTH + DIL_GROUPS * QKV_WIDTH
    for c0 in range(0, Z_WIDTH, 512):
        zacc = proj(hb, col + c0, 512)
        z_ref[:, c0:c0 + 512] = (zacc * _sigmoid(zacc)).astype(BF16)
    col += Z_WIDTH
    for c0 in range(0, G_WIDTH, 512):
        gate_ref[:, c0:c0 + 512] = _sigmoid(proj(hb, col + c0, 512)).astype(BF16)


def _in_proj(x2, pos_col, g_pre, w_cat, freqs, qg, kvg, wq, wkv, batch, seq, *, tm=512):
    t = x2.shape[0]
    tiles_per_seq = seq // tm
    mla_width = MLA_HEADS * MLA_HEAD_LANES
    row = lambda i: (i, 0)
    const = lambda i: (0, 0)
    resident = lambda shape: pl.BlockSpec(shape, const, pipeline_mode=pl.Buffered(1))

    def qkv_spec(d):
        return pl.BlockSpec((None, d, tm // d, QKV_WIDTH),
                            lambda i: (i // tiles_per_seq, 0, i % tiles_per_seq, 0))

    return pl.pallas_call(
        functools.partial(_in_proj_kernel, tm=tm),
        grid=(t // tm,),
        in_specs=[
            pl.BlockSpec((tm, D_MODEL), row),
            pl.BlockSpec((tm, 1), row),
            resident((1, D_MODEL)),
            resident((IN_TOTAL, D_MODEL)),
            resident((1, LANES)),
            resident((1, MLA_Q_RANK)),
            resident((1, MLA_KV_RANK)),
            resident((MLA_Q_RANK, mla_width)),
            resident((MLA_KV_RANK, mla_width)),
        ],
        out_specs=[pl.BlockSpec((tm, mla_width), row)] * 3
        + [qkv_spec(d) for d in DIL_DILATIONS]
        + [pl.BlockSpec((tm, Z_WIDTH), row), pl.BlockSpec((tm, G_WIDTH), row)],
        out_shape=[jax.ShapeDtypeStruct((t, mla_width), BF16)] * 3
        + [jax.ShapeDtypeStruct((batch, d, seq // d, QKV_WIDTH), BF16) for d in DIL_DILATIONS]
        + [jax.ShapeDtypeStruct((t, Z_WIDTH), BF16), jax.ShapeDtypeStruct((t, G_WIDTH), BF16)],
        scratch_shapes=[
            pltpu.VMEM((N_SLABS, tm, LANES), F32),
            pltpu.VMEM((3, tm, LANES), F32),
            pltpu.VMEM((tm, D_MODEL), BF16),
        ],
        compiler_params=_params(1),
        name="in_proj",
    )(x2, pos_col, g_pre, w_cat, freqs, qg, kvg, wq, wkv)


_NT = (((1,), (1,)), ((), ()))


def _mla_attn_kernel(q_ref, k_ref, v_ref, o_ref, s_scr, p_scr, m_scr, alpha_scr, acc_scr, *,
                     tk, heads_per_step):
    tq = 2 * tk
    seq = q_ref.shape[0]
    slices = [slice(hd * MLA_HEAD_LANES, (hd + 1) * MLA_HEAD_LANES)
              for hd in range(heads_per_step)]
    every = slice(0, tq)
    upper, lower = slice(0, tk), slice(tk, tq)

    for qi in range(seq // tq):
        q0 = qi * tq

        def scores(j, slot, rows=every):
            for hd, sl in enumerate(slices):
                s_scr[slot, hd, rows, :] = lax.dot_general(
                    q_ref[q0 + rows.start:q0 + rows.stop, sl], k_ref[j * tk:(j + 1) * tk, sl],
                    _NT, preferred_element_type=F32)

        def update(j, slot, rows=every, masked=False):
            for hd, sl in enumerate(slices):
                for c0 in range(rows.start, rows.stop, MLA_ROW_CHUNK):
                    rc = slice(c0, c0 + MLA_ROW_CHUNK)
                    s = s_scr[slot, hd, rc, :]
                    if masked:
                        r = lax.broadcasted_iota(jnp.int32, s.shape, 0) + (c0 - rows.start)
                        cidx = lax.broadcasted_iota(jnp.int32, s.shape, 1)
                        s = jnp.where(cidx <= r, s, NEG_BIG)
                    m = m_scr[hd, rc, :]
                    m_new = jnp.maximum(m, jnp.max(s, axis=-1, keepdims=True))
                    p_scr[hd, rc, :] = jnp.exp2((s - m_new).astype(BF16))
                    alpha_scr[hd, rc, :] = jnp.exp2(m - m_new)
                    m_scr[hd, rc, :] = m_new
                acc_scr[hd, rows, :] = alpha_scr[hd, rows, :] * acc_scr[hd, rows, :] + jnp.dot(
                    p_scr[hd, rows, :], v_ref[j * tk:(j + 1) * tk, sl],
                    preferred_element_type=F32)

        m_scr[...] = jnp.full(m_scr.shape, NEG_BIG, F32)
        acc_scr[...] = jnp.zeros(acc_scr.shape, F32)
        scores(0, 0)
        for jj in range(qi):
            j0 = 2 * jj
            scores(j0 + 1, 1)
            update(j0, 0)
            scores(j0 + 2, 0)
            update(j0 + 1, 1)

        jd = 2 * qi
        scores(jd + 1, 1, lower)
        update(jd, 0, upper, masked=True)
        update(jd, 0, lower)
        update(jd + 1, 1, lower, masked=True)
        for hd in range(heads_per_step):
            acc = acc_scr[hd]
            denom = acc[:, MLA_V:MLA_V + 1]
            o_ref[q0:q0 + tq, hd * MLA_V:(hd + 1) * MLA_V] = (acc[:, :MLA_V] / denom).astype(BF16)


def _mla_attn(q, k, v, batch, seq, *, tk=512, heads_per_step=2):
    width = MLA_HEADS * MLA_HEAD_LANES
    tq = 2 * tk
    q3, k3, v3 = (a.reshape(batch, seq, width) for a in (q, k, v))
    hw = heads_per_step * MLA_HEAD_LANES
    whole = pl.BlockSpec((None, seq, hw), lambda b, hp: (b, 0, hp))
    out = pl.pallas_call(
        functools.partial(_mla_attn_kernel, tk=tk, heads_per_step=heads_per_step),
        grid=(batch, MLA_HEADS // heads_per_step),
        in_specs=[whole, whole, whole],
        out_specs=pl.BlockSpec((None, seq, heads_per_step * MLA_V), lambda b, hp: (b, 0, hp)),
        out_shape=jax.ShapeDtypeStruct((batch, seq, MLA_WIDTH), BF16),
        scratch_shapes=[
            pltpu.VMEM((2, heads_per_step, tq, tk), F32),
            pltpu.VMEM((heads_per_step, tq, tk), BF16),
            pltpu.VMEM((heads_per_step, tq, 1), F32),
            pltpu.VMEM((heads_per_step, tq, 1), F32),
            pltpu.VMEM((heads_per_step, tq, MLA_HEAD_LANES), F32),
        ],
        compiler_params=_params(2),
        name="mla_attn",
    )(q3, k3, v3)
    return out.reshape(batch * seq, MLA_WIDTH)


def _dil_attn_kernel(q_ref, kp_ref, kc_ref, vp_ref, vc_ref, o_ref, lse_ref,
                     kk_scr, vv_scr, *, nb, nr):
    chunk = pl.program_id(2)
    blk = DIL_BLOCK
    kk_scr[:, :blk, :] = kp_ref[...]
    kk_scr[:, blk:, :] = kc_ref[...]
    for pair in range(DIL_HPG // 2):
        src = slice(pair * LANES, (pair + 1) * LANES)
        dst = slice(2 * pair * LANES, (2 * pair + 1) * LANES)
        one = slice((2 * pair + 1) * LANES, (2 * pair + 2) * LANES)
        vv_scr[:, :blk, dst] = vp_ref[:, :, src]
        vv_scr[:, blk:, dst] = vc_ref[:, :, src]
        vv_scr[:, :, one] = jnp.ones(vv_scr.shape[:2] + (LANES,), BF16)

    r = lax.broadcasted_iota(jnp.int32, (blk, 2 * blk), 0)
    cidx = lax.broadcasted_iota(jnp.int32, (blk, 2 * blk), 1)
    dist = r + blk - cidx
    band = jnp.where((dist >= 0) & (dist <= blk), 0.0, NEG_BIG)
    first_band = jnp.where((cidx >= blk) | (chunk > 0), band, NEG_BIG)
    lane = lax.broadcasted_iota(jnp.int32, (blk, LANES), 1)
    low = lane < DIL_DH

    for ri in range(nr):
        for j in range(nb):
            rows = slice(j * blk, (j + 1) * blk)
            band_rows = slice(j * blk, (j + 2) * blk)
            bias = first_band if j == 0 else band
            for pair in range(DIL_HPG // 2):
                cols = slice(pair * LANES, (pair + 1) * LANES)
                q2 = q_ref[ri, rows, cols]
                k2 = kk_scr[ri, band_rows, cols]
                v2 = vv_scr[ri, band_rows, 2 * pair * LANES:(2 * pair + 2) * LANES]
                qs = jnp.concatenate([jnp.where(low, q2, jnp.zeros_like(q2)),
                                      jnp.where(low, jnp.zeros_like(q2), q2)], axis=0)
                s = lax.dot_general(qs, k2, _NT, preferred_element_type=F32)
                s = s + jnp.concatenate([bias, bias], axis=0)
                m = jnp.max(s, axis=-1, keepdims=True)
                p = jnp.exp2(s - m)
                pv = jnp.dot(p.astype(BF16), v2, preferred_element_type=F32)
                num = jnp.where(low, pv[:blk, :LANES], pv[blk:, :LANES])
                denom = jnp.where(low, pv[:blk, LANES:], pv[blk:, LANES:])
                m_sel = jnp.where(low, m[:blk], m[blk:])
                o_ref[ri, rows, cols] = (num / denom).astype(BF16)
                lse_ref[ri, rows, cols] = m_sel + jnp.log2(denom)


def _dil_attn(qkv, group):
    batch, d, sub_len, _ = qkv.shape
    nb = DIL_BLOCKS_PER_STEP[group]
    nr = DIL_SUBSEQ_PER_STEP[group]
    blk = DIL_BLOCK
    rows = nb * blk
    cur = lambda part: pl.BlockSpec((None, nr, rows, DIL_WIDTH),
                                    lambda b, r, n: (b, r, n, part))
    prev = lambda part: pl.BlockSpec((None, nr, blk, DIL_WIDTH),
                                     lambda b, r, n: (b, r, jnp.maximum(n * nb - 1, 0), part))
    out = pl.BlockSpec((None, nr, rows, DIL_WIDTH), lambda b, r, n: (b, r, n, 0))
    return pl.pallas_call(
        functools.partial(_dil_attn_kernel, nb=nb, nr=nr),
        grid=(batch, d // nr, sub_len // rows),
        in_specs=[cur(0), prev(1), cur(1), prev(2), cur(2)],
        out_specs=[out, out],
        out_shape=[
            jax.ShapeDtypeStruct((batch, d, sub_len, DIL_WIDTH), BF16),
            jax.ShapeDtypeStruct((batch, d, sub_len, DIL_WIDTH), F32),
        ],
        scratch_shapes=[pltpu.VMEM((nr, rows + blk, DIL_WIDTH), BF16),
                        pltpu.VMEM((nr, rows + blk, 2 * DIL_WIDTH), BF16)],
        compiler_params=_params(3),
        name=f"dil_attn_{group}",
    )(qkv, qkv, qkv, qkv, qkv)


def _out_kernel(x_ref, ymla_ref, o0_ref, o1_ref, o2_ref, l0_ref, l1_ref, l2_ref,
                z_ref, gate_ref, wpm_ref, wpd_ref, wo_ref, pg_ref, out_ref, perm_scr, *, tm):
    n_sl = DIL_WIDTH // LANES

    def to_token_order(ref, d, slot):
        rows = tm // d
        for r in range(d):
            for s in range(n_sl):
                perm_scr[slot * n_sl + s, pl.ds(r, rows, stride=d), :] = (
                    ref[r, :, s * LANES:(s + 1) * LANES].astype(F32))

    def token_rows(slot, rs):
        return jnp.concatenate([perm_scr[slot * n_sl + s, rs, :] for s in range(n_sl)], axis=1)

    d1, d2 = DIL_DILATIONS[1], DIL_DILATIONS[2]
    to_token_order(o1_ref, d1, 0)
    to_token_order(l1_ref, d1, 1)
    to_token_order(o2_ref, d2, 2)
    to_token_order(l2_ref, d2, 3)

    for r0 in range(0, tm, tm // 2):
        rs = slice(r0, r0 + tm // 2)
        o0, l0 = o0_ref[0, rs, :].astype(F32), l0_ref[0, rs, :]
        o1, l1 = token_rows(0, rs), token_rows(1, rs)
        o2, l2 = token_rows(2, rs), token_rows(3, rs)
        m = jnp.maximum(jnp.maximum(l0, l1), l2)
        e0, e1, e2 = jnp.exp2(l0 - m), jnp.exp2(l1 - m), jnp.exp2(l2 - m)
        y_dil = (e0 * o0 + e1 * o1 + e2 * o2) / (e0 + e1 + e2)

        a = (ymla_ref[rs, :].astype(F32) * z_ref[rs, :MLA_WIDTH].astype(F32)).astype(BF16)
        b = (y_dil * z_ref[rs, MLA_WIDTH:].astype(F32)).astype(BF16)
        pa = jnp.dot(a, wpm_ref[...], preferred_element_type=F32)
        pb = jnp.dot(b, wpd_ref[...], preferred_element_type=F32)
        merged = (gate_ref[rs, :D_MODEL].astype(F32) * pa
                  + gate_ref[rs, D_MODEL:].astype(F32) * pb)
        u = jnp.dot(merged.astype(BF16), wo_ref[...], preferred_element_type=F32)
        out_ref[rs, :] = x_ref[rs, :] + _rms(u, pg_ref[...])


def _out_block(x2, y_mla, o_dil, lse_dil, z, gate, wpm, wpd, wo, pg, seq, *, tm=512):
    t = x2.shape[0]
    tiles_per_seq = seq // tm
    row = lambda i: (i, 0)
    const = lambda i: (0, 0)
    rows = lambda w: pl.BlockSpec((tm, w), row)
    sub = lambda d: pl.BlockSpec((None, d, tm // d, DIL_WIDTH),
                                 lambda i: (i // tiles_per_seq, 0, i % tiles_per_seq, 0))
    subs = [sub(d) for d in DIL_DILATIONS]
    return pl.pallas_call(
        functools.partial(_out_kernel, tm=tm),
        grid=(t // tm,),
        in_specs=[rows(D_MODEL), rows(MLA_WIDTH)] + subs + subs
        + [rows(Z_WIDTH), rows(G_WIDTH),
           pl.BlockSpec((MLA_WIDTH, D_MODEL), const),
           pl.BlockSpec((DIL_WIDTH, D_MODEL), const),
           pl.BlockSpec((D_MODEL, D_MODEL), const),
           pl.BlockSpec((1, D_MODEL), const)],
        out_specs=rows(D_MODEL),
        out_shape=jax.ShapeDtypeStruct((t, D_MODEL), F32),
        scratch_shapes=[pltpu.VMEM((4 * DIL_WIDTH // LANES, tm, LANES), F32)],
        compiler_params=_params(1),
        name="out_block",
    )(x2, y_mla, *o_dil, *lse_dil, z, gate, wpm, wpd, wo, pg)


def _inv_freq(rot_dim):
    return ROPE_THETA ** (-jnp.arange(0, rot_dim, 2, dtype=F32) / rot_dim)


def _lane_freqs():
    f = jnp.zeros((LANES,), F32)
    f = f.at[:DIL_ROPE].set(jnp.tile(_inv_freq(DIL_ROPE), 2))
    f = f.at[MLA_NOPE:MLA_NOPE + MLA_ROPE].set(jnp.tile(_inv_freq(MLA_ROPE), 2))
    return f[None, :]


def _w_cat_kernel(w_ref, o_ref):
    lat = MLA_Q_RANK + MLA_KV_RANK
    cols = w_ref.shape[1]
    o_ref[:lat, :] = w_ref[:lat, :].astype(BF16)
    o_ref[lat:lat + MLA_NOPE, :] = jnp.zeros((MLA_NOPE, cols), BF16)
    o_ref[lat + MLA_NOPE:lat + MLA_NOPE + MLA_ROPE, :] = w_ref[lat:lat + MLA_ROPE, :].astype(BF16)
    o_ref[lat + MLA_NOPE + MLA_ROPE:LAT_WIDTH, :] = jnp.zeros(
        (LANES - MLA_NOPE - MLA_ROPE, cols), BF16)
    o_ref[LAT_WIDTH:, :] = w_ref[lat + MLA_ROPE:, :].astype(BF16)


def _w_cat(w_in_t, layer, *, cols=128):
    _, n, k = w_in_t.shape
    return pl.pallas_call(
        _w_cat_kernel,
        grid=(k // cols,),
        in_specs=[pl.BlockSpec((None, n, cols), lambda i: (layer, 0, i))],
        out_specs=pl.BlockSpec((IN_TOTAL, cols), lambda i: (0, i)),
        out_shape=jax.ShapeDtypeStruct((IN_TOTAL, k), BF16),
        compiler_params=_params(1),
        name="w_cat",
    )(w_in_t)


def _layer_weights(w_in, layer, w_uq, w_ukv):
    w_cat = _w_cat(jnp.swapaxes(w_in, 1, 2), layer)

    wq = w_uq.reshape(MLA_Q_RANK, MLA_HEADS, MLA_NOPE + MLA_ROPE)
    wq = jnp.pad(wq, ((0, 0), (0, 0), (0, MLA_HEAD_LANES - MLA_NOPE - MLA_ROPE)))
    wq = wq.reshape(MLA_Q_RANK, MLA_HEADS * MLA_HEAD_LANES).astype(BF16)
    return w_cat, wq, w_ukv.astype(BF16)


def kernel(x, positions, pre_norm_g, w_in, q_norm_g, w_uq, kv_norm_g, w_ukv,
           w_proj_mla, w_proj_dil, w_out, post_norm_g):
    batch, seq, _ = x.shape
    depth = w_in.shape[0]
    pos_col = positions.reshape(batch * seq, 1)
    freqs = _lane_freqs()
    x2 = x.reshape(batch * seq, D_MODEL)
    for layer in range(depth):
        w_cat, wq, wkv = _layer_weights(w_in, layer, w_uq[layer], w_ukv[layer])
        q, k, v, qkv0, qkv1, qkv2, z, gate = _in_proj(
            x2, pos_col, pre_norm_g[layer][None, :], w_cat, freqs,
            q_norm_g[layer][None, :], kv_norm_g[layer][None, :], wq, wkv, batch, seq)
        y_mla = _mla_attn(q, k, v, batch, seq)
        dil = [_dil_attn(qkv, g) for g, qkv in enumerate((qkv0, qkv1, qkv2))]
        x2 = _out_block(x2, y_mla, [o for o, _ in dil], [l for _, l in dil], z, gate,
                        w_proj_mla[layer].astype(BF16), w_proj_dil[layer].astype(BF16),
                        w_out[layer].astype(BF16), post_norm_g[layer][None, :], seq)
    return x2.reshape(batch, seq, D_MODEL)
```
